```python
import jax, jax.numpy as jnp
from jax import lax
import numpy as np

D_MODEL = 1024
BATCH = 8
SEQ = 2048
DEPTH = 2
DEC_BATCH = 128
DEC_SEQ = 8
PAST_LEN = 16384
PAGE_SIZE = 128

N_EVEN = (DEPTH + 1) // 2
N_ODD = DEPTH // 2
D_A = D_MODEL
CONV_W = 31
H_B = 8
DK_B = 128
DV_B = 128
D_B = H_B * DV_B
CHUNK_B = 64
D_C = 2 * D_MODEL
H_C = 8
DG_C = D_C // H_C
CHUNK_C = 128
EPS = 1e-6
HK_B = H_B * DK_B
D_IN_AB = 3 * D_A + 2 * HK_B + 2 * D_B
SPLIT_AB = [D_A, 2 * D_A, 3 * D_A, 3 * D_A + HK_B, 3 * D_A + 2 * HK_B, 3 * D_A + 2 * HK_B + D_B]

kernel_name = 'hybrid_conformerconv_hgrn2_gmlp_decode_step'


def _rmsnorm(x, g):
    xf = x.astype(jnp.float32)
    y = xf * lax.rsqrt(jnp.mean(xf * xf, axis=-1, keepdims=True) + EPS)
    return (y * g.astype(jnp.float32)).astype(x.dtype)


def _layernorm(x, g, b):
    xf = x.astype(jnp.float32)
    xc = xf - jnp.mean(xf, axis=-1, keepdims=True)
    y = xc * lax.rsqrt(jnp.mean(xc * xc, axis=-1, keepdims=True) + EPS)
    return (y * g.astype(jnp.float32) + b.astype(jnp.float32)).astype(x.dtype)


def _conformer_conv(a_val, a_glu, buf, conv_w, conv_b, ln_g, ln_b):
    a = a_val * jax.nn.sigmoid(a_glu)
    xp = jnp.concatenate([buf.astype(a.dtype), a], axis=1)
    y = lax.conv_general_dilated(xp, conv_w[:, None, :].astype(a.dtype), window_strides=(1,),
                                 padding='VALID', dimension_numbers=('NWC', 'WIO', 'NWC'),
                                 feature_group_count=D_A)
    y = jax.nn.silu(_layernorm(y + conv_b.astype(a.dtype), ln_g, ln_b))
    return y, xp[:, xp.shape[1] - (CONV_W - 1):]


def _hgrn2(q, f_pre, i_in, s0, lb):
    B, L = q.shape[0], q.shape[1]
    c = min(CHUNK_B, L)
    n = -(-L // c)
    pad = n * c - L
    f = lb + (1.0 - lb) * jax.nn.sigmoid(f_pre.astype(jnp.float32))
    logf = jnp.log(f)
    k = 1.0 - f
    qf = q.astype(jnp.float32)
    vf = i_in.astype(jnp.float32)
    padw = ((0, 0), (0, pad), (0, 0), (0, 0))
    qf, logf, k, vf = [jnp.pad(t, padw).reshape(B, n, c, H_B, t.shape[-1]) for t in (qf, logf, k, vf)]
    bcum = jnp.cumsum(logf, axis=2)
    b_mid = bcum[:, :, c // 2:c // 2 + 1]
    b_end = bcum[:, :, c - 1:c]
    scores = jnp.einsum('bnthd,bnshd->bnhts', qf * jnp.exp(bcum - b_mid), k * jnp.exp(b_mid - bcum))
    causal = jnp.tril(jnp.ones((c, c), dtype=bool))
    scores = jnp.where(causal, scores, 0.0)
    o_intra = jnp.einsum('bnhts,bnshv->bnthv', scores, vf)
    ds = jnp.einsum('bnshd,bnshv->bnhdv', k * jnp.exp(b_end - bcum), vf)
    decay = jnp.exp(b_end[:, :, 0])

    def step(s, inp):
        dec, d = inp
        return dec[..., None] * s + d, s

    s_fin, s_prev = lax.scan(step, s0.astype(jnp.float32),
                             (jnp.moveaxis(decay, 1, 0), jnp.moveaxis(ds, 1, 0)))
    s_prev = jnp.moveaxis(s_prev, 0, 1)
    o_inter = jnp.einsum('bnthd,bnhdv->bnthv', qf * jnp.exp(bcum), s_prev)
    o = (o_intra + o_inter).reshape(B, n * c, H_B, DV_B)[:, :L]
    return o, s_fin


def _even_layer(x, buf, s0, g_norm, w_in, conv_w, conv_b, ln_g, ln_b, lb, o_g, w_out):
    B, L, _ = x.shape
    z = _rmsnorm(x, g_norm) @ w_in
    a_val, a_glu, a_gate, q, f_pre, i_in, b_gate = jnp.split(z, SPLIT_AB, axis=-1)
    a_out, new_buf = _conformer_conv(a_val, a_glu, buf, conv_w, conv_b, ln_g, ln_b)
    a_out = a_out * jax.nn.silu(a_gate)
    o, s_new = _hgrn2(q.reshape(B, L, H_B, DK_B), f_pre.reshape(B, L, H_B, DK_B),
                      jax.nn.silu(i_in).reshape(B, L, H_B, DV_B), s0, lb)
    o = _rmsnorm(o, o_g).reshape(B, L, D_B).astype(x.dtype) * jax.nn.silu(b_gate)
    y = jnp.concatenate([a_out, o], axis=-1) @ w_out
    return x + y, new_buf, s_new


def _odd_layer(x, g_norm, w_in, ln_g, ln_b, w_s, b_s, w_out):
    B, L, _ = x.shape
    z = _rmsnorm(x, g_norm) @ w_in
    uv = jax.nn.gelu(z[..., :2 * D_C])
    gate = z[..., 2 * D_C:]
    u = uv[..., :D_C]
    v = _layernorm(uv[..., D_C:], ln_g, ln_b)
    n = -(-L // CHUNK_C)
    pad = n * CHUNK_C - L
    vp = jnp.pad(v, ((0, 0), (0, pad), (0, 0))).reshape(B, n, CHUNK_C, H_C, DG_C)
    ws = jnp.where(jnp.tril(jnp.ones((CHUNK_C, CHUNK_C), dtype=bool)), w_s, 0.0)
    mix = jnp.einsum('hts,bnshd->bnthd', ws.astype(vp.dtype), vp) + b_s.T[None, None, :, :, None]
    mix = mix.reshape(B, n * CHUNK_C, D_C)[:, :L]
    y = (u * mix * jax.nn.silu(gate)) @ w_out
    start = ((L - 1) // CHUNK_C) * CHUNK_C
    return x + y, v[:, start:]


def _trunk(x, bufs, states, norm_ab, w_in_ab, conv_w, conv_b, ln_a_g, ln_a_b, lb_logits, onorm_b,
           w_out_ab, norm_c, w_in_c, ln_c_g, ln_c_b, w_s, b_s, w_out_c, final_norm):
    lb_all = jnp.cumsum(jax.nn.softmax(lb_logits.astype(jnp.float32), axis=0), axis=0)
    new_bufs, new_states, new_v = [], [], []
    for l in range(DEPTH):
        j = l // 2
        if l % 2 == 0:
            x, nb, ns = _even_layer(x, bufs[j], states[j], norm_ab[j], w_in_ab[j], conv_w[j], conv_b[j],
                                    ln_a_g[j], ln_a_b[j], lb_all[j].reshape(H_B, DK_B), onorm_b[j],
                                    w_out_ab[j])
            new_bufs.append(nb)
            new_states.append(ns.astype(x.dtype))
        else:
            x, nv = _odd_layer(x, norm_c[j], w_in_c[j], ln_c_g[j], ln_c_b[j], w_s[j], b_s[j], w_out_c[j])
            new_v.append(nv)
    return _rmsnorm(x, final_norm), jnp.stack(new_bufs), jnp.stack(new_states), jnp.stack(new_v)


def setup_inputs(seed: int = 0) -> dict:
    key = jax.random.key(seed)
    ks = jax.random.split(key, 24)
    nrm = lambda k, s: jax.random.normal(k, s, jnp.float32)
    return {
        'x_prompt': nrm(ks[0], (BATCH, SEQ, D_MODEL)),
        'x_sample': nrm(ks[1], (DEC_BATCH, DEC_SEQ, D_MODEL)),
        'state_conv': 0.5 * nrm(ks[2], (N_EVEN, DEC_BATCH, CONV_W - 1, D_A)),
        'state_hgrn': 0.5 * nrm(ks[3], (N_EVEN, DEC_BATCH, H_B, DK_B, DV_B)),
        'norm_ab': 1.0 + 0.02 * nrm(ks[4], (N_EVEN, D_MODEL)),
        'w_in_ab': nrm(ks[5], (N_EVEN, D_MODEL, D_IN_AB)) * D_MODEL ** -0.5,
        'conv_w': nrm(ks[6], (N_EVEN, CONV_W, D_A)) * CONV_W ** -0.5,
        'conv_b': 0.02 * nrm(ks[7], (N_EVEN, D_A)),
        'ln_a_g': 1.0 + 0.02 * nrm(ks[8], (N_EVEN, D_A)),
        'ln_a_b': 0.02 * nrm(ks[9], (N_EVEN, D_A)),
        'lb_logits': 0.1 * nrm(ks[10], (N_EVEN + 1, HK_B)),
        'onorm_b': 1.0 + 0.02 * nrm(ks[11], (N_EVEN, H_B, DV_B)),
        'w_out_ab': nrm(ks[12], (N_EVEN, D_A + D_B, D_MODEL)) * (D_A + D_B) ** -0.5,
        'norm_c': 1.0 + 0.02 * nrm(ks[13], (N_ODD, D_MODEL)),
        'w_in_c': nrm(ks[14], (N_ODD, D_MODEL, 3 * D_C)) * D_MODEL ** -0.5,
        'ln_c_g': 1.0 + 0.02 * nrm(ks[15], (N_ODD, D_C)),
        'ln_c_b': 0.02 * nrm(ks[16], (N_ODD, D_C)),
        'w_s': nrm(ks[17], (N_ODD, H_C, CHUNK_C, CHUNK_C)) * CHUNK_C ** -0.5,
        'b_s': 1.0 + 0.1 * nrm(ks[18], (N_ODD, H_C, CHUNK_C)),
        'w_out_c': nrm(ks[19], (N_ODD, D_C, D_MODEL)) * D_C ** -0.5,
        'final_norm': 1.0 + 0.02 * nrm(ks[20], (D_MODEL,)),
    }


def reference(x_prompt, x_sample, state_conv, state_hgrn, norm_ab, w_in_ab, conv_w, conv_b, ln_a_g,
              ln_a_b, lb_logits, onorm_b, w_out_ab, norm_c, w_in_c, ln_c_g, ln_c_b, w_s, b_s, w_out_c,
              final_norm):
    weights = (norm_ab, w_in_ab, conv_w, conv_b, ln_a_g, ln_a_b, lb_logits, onorm_b, w_out_ab,
               norm_c, w_in_c, ln_c_g, ln_c_b, w_s, b_s, w_out_c, final_norm)
    bp = x_prompt.shape[0]
    bufs0 = jnp.zeros((N_EVEN, bp, CONV_W - 1, D_A), x_prompt.dtype)
    states0 = jnp.zeros((N_EVEN, bp, H_B, DK_B, DV_B), jnp.float32)
    y_prompt, conv_prompt, hgrn_prompt, gmlp_v_prompt = _trunk(x_prompt, bufs0, states0, *weights)
    y_sample, conv_sample, hgrn_sample, gmlp_v_sample = _trunk(x_sample, state_conv, state_hgrn, *weights)
    return (y_prompt, y_sample, conv_prompt, hgrn_prompt, gmlp_v_prompt, conv_sample, hgrn_sample, gmlp_v_sample)
```

```python
import functools

import jax
import jax.numpy as jnp
from jax import lax
from jax.experimental import pallas as pl
from jax.experimental.pallas import tpu as pltpu

D_MODEL = 1024
D_A = D_MODEL
CONV_W = 31
CONV_HIST = CONV_W - 1
H_B = 8
DK_B = 128
DV_B = 128
D_B = H_B * DV_B
HK_B = H_B * DK_B
CHUNK_B = 64
D_C = 2 * D_MODEL
H_C = 8
DG_C = D_C // H_C
CHUNK_C = 128
EPS = 1e-6

SUBLANES = 8
LANES = 128
HIST_ROWS = 32
ROW_TILE = 256
VMEM_LIMIT_BYTES = 60 * 1024 * 1024

_F32 = jnp.float32
_BF16 = jnp.bfloat16

_NT = (((1,), (1,)), ((), ()))
_TN = (((0,), (0,)), ((), ()))


def _dot(a, b):
    return jnp.dot(a, b, preferred_element_type=_F32)


def _dot_nt(a, b):
    return lax.dot_general(a, b, _NT, preferred_element_type=_F32)


def _dot_tn(a, b):
    return lax.dot_general(a, b, _TN, preferred_element_type=_F32)


def _rms(x, g):
    return x * lax.rsqrt(jnp.mean(x * x, axis=-1, keepdims=True) + EPS) * g


def _layernorm(x, g, b):
    xc = x - jnp.mean(x, axis=-1, keepdims=True)
    return xc * lax.rsqrt(jnp.mean(xc * xc, axis=-1, keepdims=True) + EPS) * g + b


def _silu(x):
    return x * jax.nn.sigmoid(x)


def _split3(x):
    hi = x.astype(_BF16)
    r1 = x - hi.astype(_F32)
    mid = r1.astype(_BF16)
    lo = (r1 - mid.astype(_F32)).astype(_BF16)
    return hi, mid, lo


def _dot_exact_lhs(m, x):
    hi, mid, lo = _split3(x)
    return _dot(m, hi) + _dot(m, mid) + _dot(m, lo)


def _forget_lower_bound(lbl_ref):
    l = lbl_ref[...]
    e = jnp.exp(l - jnp.max(l, axis=0, keepdims=True))
    return e[0:1] / jnp.sum(e, axis=0, keepdims=True)


def _conv_tap_groups():
    return [[k for k in range(CONV_W) if k % SUBLANES == s] for s in range(SUBLANES)]


def _l0_prompt_kernel(x_ref, gn_ref, win_ref, cw_ref, cb_ref, lng_ref, lnb_ref, lbl_ref, og_ref, wout_ref,
                      y_ref, conv_ref, hgrn_ref,
                      abuf, st_s, xn_s, yc_s, q_s, lf_s, k_s, v_s, o_s, cat_s, *, T):
    t = pl.program_id(1)
    nt = pl.num_programs(1)

    @pl.when(t == 0)
    def _():
        abuf[0:HIST_ROWS] = jnp.zeros((HIST_ROWS, D_A), _F32)
        st_s[...] = jnp.zeros_like(st_s)

    x = x_ref[0]
    xn_s[...] = _rms(x, gn_ref[...]).astype(_BF16)

    a_val = _dot(xn_s[...], win_ref[:, 0:D_A])
    a_glu = _dot(xn_s[...], win_ref[:, D_A:2 * D_A])
    abuf[HIST_ROWS:HIST_ROWS + T] = a_val * jax.nn.sigmoid(a_glu)

    off0 = HIST_ROWS - CONV_HIST
    RB = 64
    WIN = RB + HIST_ROWS
    for rb in range(T // RB):
        for lb in range(D_A // LANES):
            ls = slice(lb * LANES, (lb + 1) * LANES)
            win = abuf[rb * RB:rb * RB + WIN, ls]
            acc = jnp.zeros((RB, LANES), _F32)
            for s in range(SUBLANES):
                rolled = win if s == 0 else pltpu.roll(win, WIN - s, 0)
                for k in range(CONV_W):
                    if (off0 + k) % SUBLANES != s:
                        continue
                    base = off0 + k - s
                    acc = acc + cw_ref[k:k + 1, ls] * rolled[base:base + RB]
            yc_s[rb * RB:(rb + 1) * RB, ls] = acc

    yc = _layernorm(yc_s[...] + cb_ref[...], lng_ref[...], lnb_ref[...])
    a_gate = _dot(xn_s[...], win_ref[:, 2 * D_A:3 * D_A])
    cat_s[:, 0:D_A] = (_silu(yc) * _silu(a_gate)).astype(_BF16)

    c0 = 3 * D_A
    q_s[...] = _dot(xn_s[...], win_ref[:, c0:c0 + HK_B])
    lb = _forget_lower_bound(lbl_ref)
    f = lb + (1.0 - lb) * jax.nn.sigmoid(_dot(xn_s[...], win_ref[:, c0 + HK_B:c0 + 2 * HK_B]))
    lf_s[...] = jnp.log(f)
    k_s[...] = 1.0 - f
    v_s[...] = _silu(_dot(xn_s[...], win_ref[:, c0 + 2 * HK_B:c0 + 2 * HK_B + D_B])).astype(_BF16)

    C = CHUNK_B
    ri = lax.broadcasted_iota(jnp.int32, (C, C), 0)
    ci = lax.broadcasted_iota(jnp.int32, (C, C), 1)
    causal = ci <= ri
    tril = causal.astype(_BF16)
    for c in range(T // C):
        rs = slice(c * C, (c + 1) * C)
        bcum = _dot_exact_lhs(tril, lf_s[rs])
        bmid = bcum[C // 2:C // 2 + 1]
        bend = bcum[C - 1:C]
        d = bcum - bmid
        qa = q_s[rs] * jnp.exp(d)
        kb = k_s[rs] * jnp.exp(-d)
        qe = (qa * jnp.exp(bmid)).astype(_BF16)
        kc = (kb * jnp.exp(bend - bmid)).astype(_BF16)
        qa = qa.astype(_BF16)
        kb = kb.astype(_BF16)
        decay = jnp.exp(bend)
        for h in range(H_B):
            hs = slice(h * DK_B, (h + 1) * DK_B)
            vh = v_s[rs, hs]
            sc = jnp.where(causal, _dot_nt(qa[:, hs], kb[:, hs]), 0.0)
            st = st_s[h]
            o = _dot(sc.astype(_BF16), vh) + _dot_nt(qe[:, hs], st.astype(_BF16))
            st_s[h] = st * decay[:, hs] + _dot_tn(vh, kc[:, hs])
            o_s[rs, hs] = _rms(o, og_ref[:, hs])

    b_gate = _dot(xn_s[...], win_ref[:, c0 + 2 * HK_B + D_B:c0 + 2 * HK_B + 2 * D_B])
    cat_s[:, D_A:D_A + D_B] = (o_s[...] * _silu(b_gate)).astype(_BF16)

    y_ref[0] = x + _dot(cat_s[...], wout_ref[...])

    abuf[0:HIST_ROWS] = abuf[T:T + HIST_ROWS]

    @pl.when(t == nt - 1)
    def _():
        conv_ref[0, 0] = abuf[T + off0:T + HIST_ROWS]
        for h in range(H_B):
            hgrn_ref[0, 0, h] = st_s[h].T


def _const_spec(shape):
    nd = len(shape)
    return pl.BlockSpec(shape, lambda *_: (0,) * nd, pipeline_mode=pl.Buffered(1))


def _layer0_prompt(x, gn, win, cw, cb, lng, lnb, lbl, og, wout):
    B, L, _ = x.shape
    T = ROW_TILE
    d_in = win.shape[1]
    kern = functools.partial(_l0_prompt_kernel, T=T)
    return pl.pallas_call(
        kern,
        grid=(B, L // T),
        in_specs=[
            pl.BlockSpec((1, T, D_MODEL), lambda b, t: (b, t, 0)),
            _const_spec((1, D_MODEL)),
            _const_spec((D_MODEL, d_in)),
            _const_spec((HIST_ROWS, D_A)),
            _const_spec((1, D_A)),
            _const_spec((1, D_A)),
            _const_spec((1, D_A)),
            _const_spec(lbl.shape),
            _const_spec((1, D_B)),
            _const_spec((D_A + D_B, D_MODEL)),
        ],
        out_specs=[
            pl.BlockSpec((1, T, D_MODEL), lambda b, t: (b, t, 0)),
            pl.BlockSpec((1, 1, CONV_HIST, D_A), lambda b, t: (0, b, 0, 0)),
            pl.BlockSpec((1, 1, H_B, DK_B, DV_B), lambda b, t: (0, b, 0, 0, 0)),
        ],
        out_shape=[
            jax.ShapeDtypeStruct((B, L, D_MODEL), _F32),
            jax.ShapeDtypeStruct((1, B, CONV_HIST, D_A), _F32),
            jax.ShapeDtypeStruct((1, B, H_B, DK_B, DV_B), _F32),
        ],
        scratch_shapes=[
            pltpu.VMEM((T + HIST_ROWS, D_A), _F32),
            pltpu.VMEM((H_B, DV_B, DK_B), _F32),
            pltpu.VMEM((T, D_MODEL), _BF16),
            pltpu.VMEM((T, D_A), _F32),
            pltpu.VMEM((T, HK_B), _F32),
            pltpu.VMEM((T, HK_B), _F32),
            pltpu.VMEM((T, HK_B), _F32),
            pltpu.VMEM((T, D_B), _BF16),
            pltpu.VMEM((T, D_B), _F32),
            pltpu.VMEM((T, D_A + D_B), _BF16),
        ],
        compiler_params=pltpu.CompilerParams(
            dimension_semantics=("arbitrary", "arbitrary"),
            vmem_limit_bytes=VMEM_LIMIT_BYTES),
        name="layer0_prompt",
    )(x, gn, win, cw, cb, lng, lnb, lbl, og, wout)


SEQ_PER_STEP = 4
XP_ROWS = 40


def _l0_sample_kernel(x_ref, sc_ref, sh_ref, gn_ref, win_ref, cw_ref, cb_ref, lng_ref, lnb_ref, lbl_ref,
                      og_ref, wout_ref,
                      y_ref, conv_ref, hgrn_ref,
                      xn_s, a_s, xp_s, yc_s, qe_s, kc_s, v_s, lf_s, o_s, cat_s, *, T, LS):
    j = pl.program_id(1)
    nj = pl.num_programs(1)
    G = T // LS
    GB = LANES // LS

    @pl.when(j == 0)
    def _():
        x = x_ref[...]
        xn_s[...] = _rms(x, gn_ref[...]).astype(_BF16)
        a_val = _dot(xn_s[...], win_ref[:, 0:D_A])
        a_glu = _dot(xn_s[...], win_ref[:, D_A:2 * D_A])
        a_s[...] = a_val * jax.nn.sigmoid(a_glu)

        def conv_seq(g, carry):
            r = pl.multiple_of(g * LS, LS)
            xp_s[0:CONV_HIST] = sc_ref[0, g]
            xp_s[CONV_HIST:CONV_HIST + LS] = a_s[pl.ds(r, LS)]
            conv_ref[0, g] = xp_s[LS:LS + CONV_HIST]
            for lb in range(D_A // LANES):
                ls = slice(lb * LANES, (lb + 1) * LANES)
                win = xp_s[:, ls]
                acc = jnp.zeros((LS, LANES), _F32)
                for s in range(SUBLANES):
                    rolled = win if s == 0 else pltpu.roll(win, XP_ROWS - s, 0)
                    for k in range(CONV_W):
                        if k % SUBLANES != s:
                            continue
                        acc = acc + cw_ref[k:k + 1, ls] * rolled[k - s:k - s + LS]
                yc_s[pl.ds(r, LS), ls] = acc
            return carry

        xp_s[CONV_HIST + LS:XP_ROWS] = jnp.zeros((XP_ROWS - CONV_HIST - LS, D_A), _F32)
        lax.fori_loop(0, G, conv_seq, 0)

        yc = _layernorm(yc_s[...] + cb_ref[...], lng_ref[...], lnb_ref[...])
        a_gate = _dot(xn_s[...], win_ref[:, 2 * D_A:3 * D_A])
        cat_s[:, 0:D_A] = (_silu(yc) * _silu(a_gate)).astype(_BF16)

        c0 = 3 * D_A
        q = _dot(xn_s[...], win_ref[:, c0:c0 + HK_B])
        lb = _forget_lower_bound(lbl_ref)
        f = lb + (1.0 - lb) * jax.nn.sigmoid(_dot(xn_s[...], win_ref[:, c0 + HK_B:c0 + 2 * HK_B]))
        lf_s[...] = jnp.log(f)
        kk = 1.0 - f
        v_s[...] = _silu(_dot(xn_s[...], win_ref[:, c0 + 2 * HK_B:c0 + 2 * HK_B + D_B]))

        ri = lax.broadcasted_iota(jnp.int32, (LANES, LANES), 0)
        ci = lax.broadcasted_iota(jnp.int32, (LANES, LANES), 1)
        same = (ri ^ ci) < LS
        causal = same & (ci <= ri)
        m_cum = causal.astype(_BF16)
        m_mid = (same & ((ci & (LS - 1)) <= LS // 2)).astype(_BF16)
        m_end = same.astype(_BF16)
        for gr in range(T // LANES):
            rs = slice(gr * LANES, (gr + 1) * LANES)
            hi, mid, lo = _split3(lf_s[rs])
            ex = lambda m: _dot(m, hi) + _dot(m, mid) + _dot(m, lo)
            bcum, bmid, bend = ex(m_cum), ex(m_mid), ex(m_end)
            d = bcum - bmid
            qa = q[rs] * jnp.exp(d)
            kb = kk[rs] * jnp.exp(-d)
            qe_s[rs] = qa * jnp.exp(bmid)
            kc_s[rs] = kb * jnp.exp(bend - bmid)
            qa = qa.astype(_BF16)
            kb = kb.astype(_BF16)
            for h in range(H_B):
                hs = slice(h * DK_B, (h + 1) * DK_B)
                sc = jnp.where(causal, _dot_nt(qa[:, hs], kb[:, hs]), 0.0)
                o_s[rs, hs] = _dot(sc.astype(_BF16), v_s[rs, hs].astype(_BF16))

    ones = jnp.ones((LS, LANES), _BF16)
    for u in range(SEQ_PER_STEP):
        r = pl.multiple_of((j * SEQ_PER_STEP + u) * LS, LS)
        qe = qe_s[pl.ds(r, LS)].astype(_BF16)
        kc = kc_s[pl.ds(r, LS)].astype(_BF16)
        vv = v_s[pl.ds(r, LS)].astype(_BF16)
        hi, mid, lo = _split3(lf_s[pl.ds(r, LS)])
        for h in range(H_B):
            hs = slice(h * DK_B, (h + 1) * DK_B)
            s0 = sh_ref[0, u, h]
            o_s[pl.ds(r, LS), hs] += _dot(qe[:, hs], s0.astype(_BF16))
            bend_col = _dot_tn(hi[:, hs], ones) + _dot_tn(mid[:, hs], ones) + _dot_tn(lo[:, hs], ones)
            hgrn_ref[0, u, h] = jnp.exp(bend_col) * s0 + _dot_tn(kc[:, hs], vv[:, hs])

    @pl.when(j == nj - 1)
    def _():
        c0 = 3 * D_A + 2 * HK_B + D_B
        b_gate = _dot(xn_s[...], win_ref[:, c0:c0 + D_B])
        for h in range(H_B):
            hs = slice(h * DV_B, (h + 1) * DV_B)
            cat_s[:, D_A + h * DV_B:D_A + (h + 1) * DV_B] = (
                _rms(o_s[:, hs], og_ref[:, hs]) * _silu(b_gate[:, hs])).astype(_BF16)
        y_ref[...] = x_ref[...] + _dot(cat_s[...], wout_ref[...])


def _layer0_sample(x, sconv, shgrn, gn, win, cw, cb, lng, lnb, lbl, og, wout):
    NB, LS, _ = x.shape
    T = ROW_TILE
    G = T // LS
    nsteps = G // SEQ_PER_STEP
    d_in = win.shape[1]
    xr = x.reshape(NB * LS, D_MODEL)
    kern = functools.partial(_l0_sample_kernel, T=T, LS=LS)
    y, conv_new, hgrn_new = pl.pallas_call(
        kern,
        grid=(NB // G, nsteps),
        in_specs=[
            pl.BlockSpec((T, D_MODEL), lambda i, j: (i, 0)),
            pl.BlockSpec((1, G, CONV_HIST, D_A), lambda i, j: (0, i, 0, 0)),
            pl.BlockSpec((1, SEQ_PER_STEP, H_B, DK_B, DV_B), lambda i, j: (0, i * nsteps + j, 0, 0, 0)),
            _const_spec((1, D_MODEL)),
            _const_spec((D_MODEL, d_in)),
            _const_spec((HIST_ROWS, D_A)),
            _const_spec((1, D_A)),
            _const_spec((1, D_A)),
            _const_spec((1, D_A)),
            _const_spec(lbl.shape),
            _const_spec((1, D_B)),
            _const_spec((D_A + D_B, D_MODEL)),
        ],
        out_specs=[
            pl.BlockSpec((T, D_MODEL), lambda i, j: (i, 0)),
            pl.BlockSpec((1, G, CONV_HIST, D_A), lambda i, j: (0, i, 0, 0)),
            pl.BlockSpec((1, SEQ_PER_STEP, H_B, DK_B, DV_B), lambda i, j: (0, i * nsteps + j, 0, 0, 0)),
        ],
        out_shape=[
            jax.ShapeDtypeStruct((NB * LS, D_MODEL), _F32),
            jax.ShapeDtypeStruct(sconv.shape, _F32),
            jax.ShapeDtypeStruct(shgrn.shape, _F32),
        ],
        scratch_shapes=[
            pltpu.VMEM((T, D_MODEL), _BF16),
            pltpu.VMEM((T, D_A), _F32),
            pltpu.VMEM((XP_ROWS, D_A), _F32),
            pltpu.VMEM((T, D_A), _F32),
            pltpu.VMEM((T, HK_B), _F32),
            pltpu.VMEM((T, HK_B), _F32),
            pltpu.VMEM((T, D_B), _F32),
            pltpu.VMEM((T, HK_B), _F32),
            pltpu.VMEM((T, D_B), _F32),
            pltpu.VMEM((T, D_A + D_B), _BF16),
        ],
        compiler_params=pltpu.CompilerParams(
            dimension_semantics=("arbitrary", "arbitrary"),
            vmem_limit_bytes=VMEM_LIMIT_BYTES),
        name="layer0_sample",
    )(xr, sconv, shgrn, gn, win, cw, cb, lng, lnb, lbl, og, wout)
    return y.reshape(NB, LS, D_MODEL), conv_new, hgrn_new


def _l1_kernel(x_ref, gn_ref, win_ref, lng_ref, lnb_ref, wmix_ref, bmix_ref, wout_ref, fn_ref,
               y_ref, v_ref, xn_s, vb_s, ug_s, p_s, *, T, R, tiles_per_seq):
    i = pl.program_id(0)
    x = x_ref[...]
    xn_s[...] = _rms(x, gn_ref[...]).astype(_BF16)
    u = jax.nn.gelu(_dot(xn_s[...], win_ref[:, 0:D_C]))
    gate = _dot(xn_s[...], win_ref[:, 2 * D_C:3 * D_C])
    ug_s[...] = u * _silu(gate)
    v = _layernorm(jax.nn.gelu(_dot(xn_s[...], win_ref[:, D_C:2 * D_C])), lng_ref[...], lnb_ref[...])
    vb_s[...] = v.astype(_BF16)
    if tiles_per_seq is None:
        v_ref[...] = v
    else:
        @pl.when(i % tiles_per_seq == tiles_per_seq - 1)
        def _():
            v_ref[0, 0] = v[T - CHUNK_C:T]

    ri = lax.broadcasted_iota(jnp.int32, (CHUNK_C, CHUNK_C), 0)
    ci = lax.broadcasted_iota(jnp.int32, (CHUNK_C, CHUNK_C), 1)
    mask = ((ri ^ ci) < R) & (ci <= ri)
    for h in range(H_C):
        hs = slice(h * DG_C, (h + 1) * DG_C)
        wm = jnp.where(mask, wmix_ref[h], 0.0).astype(_BF16)
        for gr in range(T // CHUNK_C):
            rs = slice(gr * CHUNK_C, (gr + 1) * CHUNK_C)
            mix = _dot(wm, vb_s[rs, hs]) + bmix_ref[h]
            p_s[rs, hs] = (ug_s[rs, hs] * mix).astype(_BF16)
    y = x + _dot(p_s[...], wout_ref[...])
    y_ref[...] = _rms(y, fn_ref[...])


def _layer1(xr, gn, win, lng, lnb, wmix, bmix, wout, fn, *, R, seq_len):
    N = xr.shape[0]
    T = ROW_TILE
    if seq_len is None:
        tiles_per_seq = None
        v_shape = (N, D_C)
        v_spec = pl.BlockSpec((T, D_C), lambda i: (i, 0))
    else:
        tiles_per_seq = seq_len // T
        v_shape = (1, N // seq_len, CHUNK_C, D_C)
        v_spec = pl.BlockSpec((1, 1, CHUNK_C, D_C), lambda i: (0, i // tiles_per_seq, 0, 0))
    kern = functools.partial(_l1_kernel, T=T, R=R, tiles_per_seq=tiles_per_seq)
    return pl.pallas_call(
        kern,
        grid=(N // T,),
        in_specs=[
            pl.BlockSpec((T, D_MODEL), lambda i: (i, 0)),
            _const_spec((1, D_MODEL)),
            _const_spec((D_MODEL, 3 * D_C)),
            _const_spec((1, D_C)),
            _const_spec((1, D_C)),
            _const_spec((H_C, CHUNK_C, CHUNK_C)),
            _const_spec((H_C, CHUNK_C, 1)),
            _const_spec((D_C, D_MODEL)),
            _const_spec((1, D_MODEL)),
        ],
        out_specs=[pl.BlockSpec((T, D_MODEL), lambda i: (i, 0)), v_spec],
        out_shape=[jax.ShapeDtypeStruct((N, D_MODEL), _F32), jax.ShapeDtypeStruct(v_shape, _F32)],
        scratch_shapes=[
            pltpu.VMEM((T, D_MODEL), _BF16),
            pltpu.VMEM((T, D_C), _BF16),
            pltpu.VMEM((T, D_C), _F32),
            pltpu.VMEM((T, D_C), _BF16),
        ],
        compiler_params=pltpu.CompilerParams(
            dimension_semantics=("arbitrary",),
            vmem_limit_bytes=VMEM_LIMIT_BYTES),
        name="layer1_prompt" if seq_len is not None else "layer1_sample",
    )(xr, gn, win, lng, lnb, wmix, bmix, wout, fn)


def kernel(x_prompt, x_sample, state_conv, state_hgrn, norm_ab, w_in_ab, conv_w, conv_b, ln_a_g, ln_a_b, lb_logits, onorm_b, w_out_ab, norm_c, w_in_c, ln_c_g, ln_c_b, w_s, b_s, w_out_c, final_norm):
    B, L, _ = x_prompt.shape
    NB, LS, _ = x_sample.shape
    row = lambda p: p.reshape(1, -1)

    gn0 = row(norm_ab[0])
    win0 = w_in_ab[0].astype(_BF16)
    wout0 = w_out_ab[0].astype(_BF16)
    cw = jnp.pad(conv_w[0], ((0, HIST_ROWS - CONV_W), (0, 0)))
    cb, lng, lnb, og = row(conv_b[0]), row(ln_a_g[0]), row(ln_a_b[0]), row(onorm_b[0])
    l0_params = (gn0, win0, cw, cb, lng, lnb, lb_logits, og, wout0)

    xp1, conv_prompt, hgrn_prompt = _layer0_prompt(x_prompt, *l0_params)
    xs1, conv_sample, hgrn_sample = _layer0_sample(x_sample, state_conv, state_hgrn, *l0_params)

    gn1 = row(norm_c[0])
    win1 = w_in_c[0].astype(_BF16)
    wout1 = w_out_c[0].astype(_BF16)
    lcg, lcb, fn = row(ln_c_g[0]), row(ln_c_b[0]), row(final_norm)
    reps = CHUNK_C // LS
    wmix_p, bmix_p = w_s[0], b_s[0][:, :, None]
    wmix_s = jnp.tile(w_s[0][:, :LS, :LS], (1, reps, reps))
    bmix_s = jnp.tile(b_s[0][:, :LS], (1, reps))[:, :, None]

    y_prompt, v_prompt = _layer1(xp1.reshape(B * L, D_MODEL), gn1, win1, lcg, lcb, wmix_p, bmix_p, wout1, fn,
                                 R=CHUNK_C, seq_len=L)
    y_sample, v_sample = _layer1(xs1.reshape(NB * LS, D_MODEL), gn1, win1, lcg, lcb, wmix_s, bmix_s, wout1, fn,
                                 R=LS, seq_len=None)

    return (y_prompt.reshape(B, L, D_MODEL), y_sample.reshape(NB, LS, D_MODEL),
            conv_prompt, hgrn_prompt, v_prompt,
            conv_sample, hgrn_sample, v_sample.reshape(1, NB, LS, D_C))
```

```python
import functools

import jax
import jax.numpy as jnp
from jax import lax
from jax.experimental import pallas as pl
from jax.experimental.pallas import tpu as pltpu

D_MODEL = 1024
D_A = D_MODEL
CONV_W = 31
CONV_HIST = CONV_W - 1
H_B = 8
DK_B = 128
DV_B = 128
D_B = H_B * DV_B
HK_B = H_B * DK_B
CHUNK_B = 64
D_C = 2 * D_MODEL
H_C = 8
DG_C = D_C // H_C
CHUNK_C = 128
EPS = 1e-6

SUBLANES = 8
LANES = 128
HIST_ROWS = 32
ROW_TILE = 256
PROMPT_ROW_TILE = 512
VMEM_LIMIT_BYTES = 60 * 1024 * 1024

_F32 = jnp.float32
_BF16 = jnp.bfloat16

_NT = (((1,), (1,)), ((), ()))
_TN = (((0,), (0,)), ((), ()))


def _dot(a, b):
    return jnp.dot(a, b, preferred_element_type=_F32)


def _dot_nt(a, b):
    return lax.dot_general(a, b, _NT, preferred_element_type=_F32)


def _dot_tn(a, b):
    return lax.dot_general(a, b, _TN, preferred_element_type=_F32)


def _rms(x, g):
    return x * lax.rsqrt(jnp.mean(x * x, axis=-1, keepdims=True) + EPS) * g


def _layernorm(x, g, b):
    xc = x - jnp.mean(x, axis=-1, keepdims=True)
    return xc * lax.rsqrt(jnp.mean(xc * xc, axis=-1, keepdims=True) + EPS) * g + b


def _silu(x):
    hx = 0.5 * x
    return hx * jnp.tanh(hx) + hx


_GELU_C = 0.7978845608028654
_GELU_C3 = _GELU_C * 0.044715


def _gelu(x):
    hx = 0.5 * x
    return hx * jnp.tanh(x * (_GELU_C3 * (x * x) + _GELU_C)) + hx


def _split3(x):
    hi = x.astype(_BF16)
    r1 = x - hi.astype(_F32)
    mid = r1.astype(_BF16)
    lo = (r1 - mid.astype(_F32)).astype(_BF16)
    return hi, mid, lo


def _dot_exact_lhs(m, x):
    hi, mid, lo = _split3(x)
    return _dot(m, hi) + _dot(m, mid) + _dot(m, lo)


def _forget_lower_bound(lbl_ref):
    l = lbl_ref[...]
    e = jnp.exp(l - jnp.max(l, axis=0, keepdims=True))
    return e[0:1] / jnp.sum(e, axis=0, keepdims=True)


def _conv_tap_groups():
    return [[k for k in range(CONV_W) if k % SUBLANES == s] for s in range(SUBLANES)]


def _l0_prompt_kernel(x_ref, gn_ref, win_ref, cw_ref, cb_ref, lng_ref, lnb_ref, lbl_ref, og_ref, wout_ref,
                      y_ref, conv_ref, hgrn_ref,
                      abuf, st_s, xn_s, yc_s, z_s, lf_s, k_s, v_s, cat_s, *, T):
    t = pl.program_id(1)
    nt = pl.num_programs(1)
    CG = 256
    RBLK = 256

    @pl.when(t == 0)
    def _():
        abuf[0:HIST_ROWS] = jnp.zeros((HIST_ROWS, D_A), _F32)
        st_s[...] = jnp.zeros_like(st_s)

    xn_s[...] = _rms(x_ref[0], gn_ref[...]).astype(_BF16)
    KT = 256

    def zdot(c0):
        acc = None
        for kt in range(D_MODEL // KT):
            part = _dot(xn_s[:, kt * KT:(kt + 1) * KT], win_ref[kt * KT:(kt + 1) * KT, c0:c0 + CG])
            acc = part if acc is None else acc + part
        return acc

    for g in range(D_A // CG):
        a_val = zdot(g * CG)
        a_glu = zdot(D_A + g * CG)
        abuf[HIST_ROWS:HIST_ROWS + T, g * CG:(g + 1) * CG] = a_val * jax.nn.sigmoid(a_glu)

    off0 = HIST_ROWS - CONV_HIST
    RB = 64
    WIN = RB + HIST_ROWS

    zero_of = lambda v: jnp.minimum(jnp.abs(v), 0.0)

    def conv_block(rb, lb, dep):
        ls = slice(lb * LANES, (lb + 1) * LANES)
        win = abuf[rb * RB:rb * RB + WIN, ls]
        acc = jnp.concatenate([dep, jnp.zeros((RB - SUBLANES, LANES), _F32)], axis=0)
        for s in range(SUBLANES):
            rolled = win if s == 0 else pltpu.roll(win, WIN - s, 0)
            for k in range(CONV_W):
                if (off0 + k) % SUBLANES != s:
                    continue
                base = off0 + k - s
                acc = acc + cw_ref[k:k + 1, ls] * rolled[base:base + RB]
        yc_s[rb * RB:(rb + 1) * RB, ls] = acc
        return zero_of(acc[0:SUBLANES])

    n_lb = D_A // LANES
    n_conv = (T // RB) * n_lb
    z0 = 2 * D_A
    n_dot = (win_ref.shape[1] - z0) // CG
    emitted = 0
    dep = jnp.zeros((SUBLANES, LANES), _F32)
    dot_deps = {}
    for i in range(n_conv):
        if i - 2 in dot_deps:
            dep = dep + dot_deps.pop(i - 2)
        dep = conv_block(i // n_lb, i % n_lb, dep)
        while emitted < ((i + 1) * n_dot) // n_conv:
            z = zdot(z0 + emitted * CG)
            z_s[:, emitted * CG:(emitted + 1) * CG] = z
            dot_deps[i] = zero_of(z[0:SUBLANES, 0:LANES])
            emitted += 1
    ZG, ZQ, ZF, ZI, ZB = 0, D_A, D_A + HK_B, D_A + 2 * HK_B, D_A + 2 * HK_B + D_B

    for r in range(T // RBLK):
        rs = slice(r * RBLK, (r + 1) * RBLK)
        yc = _layernorm(yc_s[rs] + cb_ref[...], lng_ref[...], lnb_ref[...])
        cat_a = (_silu(yc) * _silu(z_s[rs, ZG:ZG + D_A])).astype(_BF16)
        y_ref[0, rs] = x_ref[0, rs] + _dot(cat_a, wout_ref[0:D_A, :])

    lb = _forget_lower_bound(lbl_ref)
    f = lb + (1.0 - lb) * jax.nn.sigmoid(z_s[:, ZF:ZF + HK_B])
    lf_s[...] = jnp.log(f)
    k_s[...] = 1.0 - f
    v_s[...] = _silu(z_s[:, ZI:ZI + D_B]).astype(_BF16)

    C = CHUNK_B
    ri = lax.broadcasted_iota(jnp.int32, (C, C), 0)
    ci = lax.broadcasted_iota(jnp.int32, (C, C), 1)
    tril = (ci <= ri).astype(_BF16)
    rg = lax.broadcasted_iota(jnp.int32, (RBLK, RBLK), 0)
    cg = lax.broadcasted_iota(jnp.int32, (RBLK, RBLK), 1)
    intra = ((rg ^ cg) < C) & (cg <= rg)
    heads = [slice(h * DK_B, (h + 1) * DK_B) for h in range(H_B)]
    for r in range(T // RBLK):
        gs = slice(r * RBLK, (r + 1) * RBLK)
        qa_l, kb_l, qe_l, kc_l, decay_l = [], [], [], [], []
        for c in range(RBLK // C):
            rs = slice(r * RBLK + c * C, r * RBLK + (c + 1) * C)
            bcum = _dot_exact_lhs(tril, lf_s[rs])
            bmid = bcum[C // 2:C // 2 + 1]
            bend = bcum[C - 1:C]
            d = bcum - bmid
            qa = z_s[rs, ZQ:ZQ + HK_B] * jnp.exp(d)
            kb = k_s[rs] * jnp.exp(-d)
            qe_l.append((qa * jnp.exp(bmid)).astype(_BF16))
            kc_l.append((kb * jnp.exp(bend - bmid)).astype(_BF16))
            qa_l.append(qa.astype(_BF16))
            kb_l.append(kb.astype(_BF16))
            decay_l.append(jnp.exp(bend))
        qa = jnp.concatenate(qa_l, axis=0)
        kb = jnp.concatenate(kb_l, axis=0)
        o_intra = []
        for hs in heads:
            sc = jnp.where(intra, _dot_nt(qa[:, hs], kb[:, hs]), 0.0).astype(_BF16)
            o_intra.append(_dot(sc, v_s[gs, hs]))
        for c in range(RBLK // C):
            rs = slice(r * RBLK + c * C, r * RBLK + (c + 1) * C)
            for h, hs in enumerate(heads):
                st = st_s[h]
                o = o_intra[h][c * C:(c + 1) * C] + _dot_nt(qe_l[c][:, hs], st.astype(_BF16))
                st_s[h] = st * decay_l[c][:, hs] + _dot_tn(v_s[rs, hs], kc_l[c][:, hs])
                gate = _silu(z_s[rs, ZB + h * DV_B:ZB + (h + 1) * DV_B])
                cat_s[rs, hs] = (_rms(o, og_ref[:, hs]) * gate).astype(_BF16)
        y_ref[0, gs] += _dot(cat_s[gs], wout_ref[D_A:D_A + D_B, :])

    abuf[0:HIST_ROWS] = abuf[T:T + HIST_ROWS]

    @pl.when(t == nt - 1)
    def _():
        conv_ref[0, 0] = abuf[T + off0:T + HIST_ROWS]
        for h in range(H_B):
            hgrn_ref[0, 0, h] = st_s[h].T


def _const_spec(shape):
    nd = len(shape)
    return pl.BlockSpec(shape, lambda *_: (0,) * nd, pipeline_mode=pl.Buffered(1))


def _layer0_prompt(x, gn, win, cw, cb, lng, lnb, lbl, og, wout):
    B, L, _ = x.shape
    T = PROMPT_ROW_TILE
    d_in = win.shape[1]
    kern = functools.partial(_l0_prompt_kernel, T=T)
    return pl.pallas_call(
        kern,
        grid=(B, L // T),
        in_specs=[
            pl.BlockSpec((1, T, D_MODEL), lambda b, t: (b, t, 0)),
            _const_spec((1, D_MODEL)),
            _const_spec((D_MODEL, d_in)),
            _const_spec((HIST_ROWS, D_A)),
            _const_spec((1, D_A)),
            _const_spec((1, D_A)),
            _const_spec((1, D_A)),
            _const_spec(lbl.shape),
            _const_spec((1, D_B)),
            _const_spec((D_A + D_B, D_MODEL)),
        ],
        out_specs=[
            pl.BlockSpec((1, T, D_MODEL), lambda b, t: (b, t, 0)),
            pl.BlockSpec((1, 1, CONV_HIST, D_A), lambda b, t: (0, b, 0, 0)),
            pl.BlockSpec((1, 1, H_B, DK_B, DV_B), lambda b, t: (0, b, 0, 0, 0)),
        ],
        out_shape=[
            jax.ShapeDtypeStruct((B, L, D_MODEL), _F32),
            jax.ShapeDtypeStruct((1, B, CONV_HIST, D_A), _F32),
            jax.ShapeDtypeStruct((1, B, H_B, DK_B, DV_B), _F32),
        ],
        scratch_shapes=[
            pltpu.VMEM((T + HIST_ROWS, D_A), _F32),
            pltpu.VMEM((H_B, DV_B, DK_B), _F32),
            pltpu.VMEM((T, D_MODEL), _BF16),
            pltpu.VMEM((T, D_A), _F32),
            pltpu.VMEM((T, d_in - 2 * D_A), _F32),
            pltpu.VMEM((T, HK_B), _F32),
            pltpu.VMEM((T, HK_B), _F32),
            pltpu.VMEM((T, D_B), _BF16),
            pltpu.VMEM((T, D_B), _BF16),
        ],
        compiler_params=pltpu.CompilerParams(
            dimension_semantics=("arbitrary", "arbitrary"),
            vmem_limit_bytes=VMEM_LIMIT_BYTES),
        name="layer0_prompt",
    )(x, gn, win, cw, cb, lng, lnb, lbl, og, wout)


SEQ_PER_STEP = 4
XP_ROWS = 40


def _l0_sample_kernel(x_ref, sc_ref, sh_ref, gn_ref, win_ref, cw_ref, cb_ref, lng_ref, lnb_ref, lbl_ref,
                      og_ref, wout_ref,
                      y_ref, conv_ref, hgrn_ref,
                      xn_s, a_s, xp_s, yc_s, qe_s, kc_s, v_s, lf_s, o_s, cat_s, *, T, LS):
    j = pl.program_id(1)
    nj = pl.num_programs(1)
    G = T // LS
    GB = LANES // LS

    @pl.when(j == 0)
    def _():
        x = x_ref[...]
        xn_s[...] = _rms(x, gn_ref[...]).astype(_BF16)
        a_val = _dot(xn_s[...], win_ref[:, 0:D_A])
        a_glu = _dot(xn_s[...], win_ref[:, D_A:2 * D_A])
        a_s[...] = a_val * jax.nn.sigmoid(a_glu)

        def conv_seq(g, carry):
            r = pl.multiple_of(g * LS, LS)
            xp_s[0:CONV_HIST] = sc_ref[0, g]
            xp_s[CONV_HIST:CONV_HIST + LS] = a_s[pl.ds(r, LS)]
            conv_ref[0, g] = xp_s[LS:LS + CONV_HIST]
            for lb in range(D_A // LANES):
                ls = slice(lb * LANES, (lb + 1) * LANES)
                win = xp_s[:, ls]
                acc = jnp.zeros((LS, LANES), _F32)
                for s in range(SUBLANES):
                    rolled = win if s == 0 else pltpu.roll(win, XP_ROWS - s, 0)
                    for k in range(CONV_W):
                        if k % SUBLANES != s:
                            continue
                        acc = acc + cw_ref[k:k + 1, ls] * rolled[k - s:k - s + LS]
                yc_s[pl.ds(r, LS), ls] = acc
            return carry

        xp_s[CONV_HIST + LS:XP_ROWS] = jnp.zeros((XP_ROWS - CONV_HIST - LS, D_A), _F32)
        lax.fori_loop(0, G, conv_seq, 0)

        yc = _layernorm(yc_s[...] + cb_ref[...], lng_ref[...], lnb_ref[...])
        a_gate = _dot(xn_s[...], win_ref[:, 2 * D_A:3 * D_A])
        cat_s[:, 0:D_A] = (_silu(yc) * _silu(a_gate)).astype(_BF16)

        c0 = 3 * D_A
        q = _dot(xn_s[...], win_ref[:, c0:c0 + HK_B])
        lb = _forget_lower_bound(lbl_ref)
        f = lb + (1.0 - lb) * jax.nn.sigmoid(_dot(xn_s[...], win_ref[:, c0 + HK_B:c0 + 2 * HK_B]))
        lf_s[...] = jnp.log(f)
        kk = 1.0 - f
        v_s[...] = _silu(_dot(xn_s[...], win_ref[:, c0 + 2 * HK_B:c0 + 2 * HK_B + D_B]))

        ri = lax.broadcasted_iota(jnp.int32, (LANES, LANES), 0)
        ci = lax.broadcasted_iota(jnp.int32, (LANES, LANES), 1)
        same = (ri ^ ci) < LS
        causal = same & (ci <= ri)
        m_cum = causal.astype(_BF16)
        m_mid = (same & ((ci & (LS - 1)) <= LS // 2)).astype(_BF16)
        m_end = same.astype(_BF16)
        for gr in range(T // LANES):
            rs = slice(gr * LANES, (gr + 1) * LANES)
            hi, mid, lo = _split3(lf_s[rs])
            ex = lambda m: _dot(m, hi) + _dot(m, mid) + _dot(m, lo)
            bcum, bmid, bend = ex(m_cum), ex(m_mid), ex(m_end)
            d = bcum - bmid
            qa = q[rs] * jnp.exp(d)
            kb = kk[rs] * jnp.exp(-d)
            qe_s[rs] = qa * jnp.exp(bmid)
            kc_s[rs] = kb * jnp.exp(bend - bmid)
            qa = qa.astype(_BF16)
            kb = kb.astype(_BF16)
            for h in range(H_B):
                hs = slice(h * DK_B, (h + 1) * DK_B)
                sc = jnp.where(causal, _dot_nt(qa[:, hs], kb[:, hs]), 0.0)
                o_s[rs, hs] = _dot(sc.astype(_BF16), v_s[rs, hs].astype(_BF16))

    rhs_sel = jnp.concatenate([jnp.zeros((3 * LS, DV_B), _F32), jnp.ones((3 * LS, DV_B), _F32)], axis=1)
    zpad = jnp.zeros((LS, DV_B), _F32)
    for u in range(SEQ_PER_STEP):
        r = pl.multiple_of((j * SEQ_PER_STEP + u) * LS, LS)
        qe = qe_s[pl.ds(r, LS)].astype(_BF16)
        vv = v_s[pl.ds(r, LS)]
        hi, mid, lo = _split3(lf_s[pl.ds(r, LS)])
        lhs = jnp.concatenate([kc_s[pl.ds(r, LS)], hi.astype(_F32), mid.astype(_F32), lo.astype(_F32)],
                              axis=0).astype(_BF16)
        for h in range(H_B):
            hs = slice(h * DK_B, (h + 1) * DK_B)
            s0 = sh_ref[0, u, h]
            o_s[pl.ds(r, LS), hs] += _dot(qe[:, hs], s0.astype(_BF16))
            rhs = jnp.concatenate([jnp.concatenate([vv[:, hs], zpad], axis=1), rhs_sel], axis=0).astype(_BF16)
            both = _dot_tn(lhs[:, hs], rhs)
            hgrn_ref[0, u, h] = jnp.exp(both[:, DV_B:]) * s0 + both[:, :DV_B]

    @pl.when(j == nj - 1)
    def _():
        c0 = 3 * D_A + 2 * HK_B + D_B
        b_gate = _dot(xn_s[...], win_ref[:, c0:c0 + D_B])
        for h in range(H_B):
            hs = slice(h * DV_B, (h + 1) * DV_B)
            cat_s[:, D_A + h * DV_B:D_A + (h + 1) * DV_B] = (
                _rms(o_s[:, hs], og_ref[:, hs]) * _silu(b_gate[:, hs])).astype(_BF16)
        y_ref[...] = x_ref[...] + _dot(cat_s[...], wout_ref[...])


def _layer0_sample(x, sconv, shgrn, gn, win, cw, cb, lng, lnb, lbl, og, wout):
    NB, LS, _ = x.shape
    T = ROW_TILE
    G = T // LS
    nsteps = G // SEQ_PER_STEP
    d_in = win.shape[1]
    xr = x.reshape(NB * LS, D_MODEL)
    kern = functools.partial(_l0_sample_kernel, T=T, LS=LS)
    y, conv_new, hgrn_new = pl.pallas_call(
        kern,
        grid=(NB // G, nsteps),
        in_specs=[
            pl.BlockSpec((T, D_MODEL), lambda i, j: (i, 0)),
            pl.BlockSpec((1, G, CONV_HIST, D_A), lambda i, j: (0, i, 0, 0)),
            pl.BlockSpec((1, SEQ_PER_STEP, H_B, DK_B, DV_B), lambda i, j: (0, i * nsteps + j, 0, 0, 0)),
            _const_spec((1, D_MODEL)),
            _const_spec((D_MODEL, d_in)),
            _const_spec((HIST_ROWS, D_A)),
            _const_spec((1, D_A)),
            _const_spec((1, D_A)),
            _const_spec((1, D_A)),
            _const_spec(lbl.shape),
            _const_spec((1, D_B)),
            _const_spec((D_A + D_B, D_MODEL)),
        ],
        out_specs=[
            pl.BlockSpec((T, D_MODEL), lambda i, j: (i, 0)),
            pl.BlockSpec((1, G, CONV_HIST, D_A), lambda i, j: (0, i, 0, 0)),
            pl.BlockSpec((1, SEQ_PER_STEP, H_B, DK_B, DV_B), lambda i, j: (0, i * nsteps + j, 0, 0, 0)),
        ],
        out_shape=[
            jax.ShapeDtypeStruct((NB * LS, D_MODEL), _F32),
            jax.ShapeDtypeStruct(sconv.shape, _F32),
            jax.ShapeDtypeStruct(shgrn.shape, _F32),
        ],
        scratch_shapes=[
            pltpu.VMEM((T, D_MODEL), _BF16),
            pltpu.VMEM((T, D_A), _F32),
            pltpu.VMEM((XP_ROWS, D_A), _F32),
            pltpu.VMEM((T, D_A), _F32),
            pltpu.VMEM((T, HK_B), _F32),
            pltpu.VMEM((T, HK_B), _F32),
            pltpu.VMEM((T, D_B), _F32),
            pltpu.VMEM((T, HK_B), _F32),
            pltpu.VMEM((T, D_B), _F32),
            pltpu.VMEM((T, D_A + D_B), _BF16),
        ],
        compiler_params=pltpu.CompilerParams(
            dimension_semantics=("arbitrary", "arbitrary"),
            vmem_limit_bytes=VMEM_LIMIT_BYTES),
        name="layer0_sample",
    )(xr, sconv, shgrn, gn, win, cw, cb, lng, lnb, lbl, og, wout)
    return y.reshape(NB, LS, D_MODEL), conv_new, hgrn_new


def _l1_kernel(x_ref, gn_ref, win_ref, lng_ref, lnb_ref, wmix_ref, bmix_ref, wout_ref, fn_ref,
               y_ref, v_ref, xn_s, vb_s, ug_s, p_s, *, T, R, tiles_per_seq):
    i = pl.program_id(0)
    x = x_ref[...]
    xn_s[...] = _rms(x, gn_ref[...]).astype(_BF16)
    u = _gelu(_dot(xn_s[...], win_ref[:, 0:D_C]))
    gate = _dot(xn_s[...], win_ref[:, 2 * D_C:3 * D_C])
    ug_s[...] = u * _silu(gate)
    v = _layernorm(_gelu(_dot(xn_s[...], win_ref[:, D_C:2 * D_C])), lng_ref[...], lnb_ref[...])
    vb_s[...] = v.astype(_BF16)
    if tiles_per_seq is None:
        v_ref[...] = v
    else:
        @pl.when(i % tiles_per_seq == tiles_per_seq - 1)
        def _():
            v_ref[0, 0] = v[T - CHUNK_C:T]

    ri = lax.broadcasted_iota(jnp.int32, (CHUNK_C, CHUNK_C), 0)
    ci = lax.broadcasted_iota(jnp.int32, (CHUNK_C, CHUNK_C), 1)
    mask = ((ri ^ ci) < R) & (ci <= ri)
    for h in range(H_C):
        hs = slice(h * DG_C, (h + 1) * DG_C)
        wg = wmix_ref[h]
        if R < CHUNK_C:
            lane = lax.broadcasted_iota(jnp.int32, wg.shape, 1)
            wg = jnp.where(lane < R, wg, 0.0)
            span = R
            while span < CHUNK_C:
                wg = wg + pltpu.roll(wg, span, 1)
                span *= 2
            wg = jnp.concatenate([wg] * (CHUNK_C // R), axis=0)
        wm = jnp.where(mask, wg, 0.0).astype(_BF16)
        for gr in range(T // CHUNK_C):
            rs = slice(gr * CHUNK_C, (gr + 1) * CHUNK_C)
            mix = _dot(wm, vb_s[rs, hs]) + bmix_ref[h]
            p_s[rs, hs] = (ug_s[rs, hs] * mix).astype(_BF16)
    y = x + _dot(p_s[...], wout_ref[...])
    y_ref[...] = _rms(y, fn_ref[...])


def _layer1(xr, gn, win, lng, lnb, wmix, bmix, wout, fn, *, R, seq_len):
    N = xr.shape[0]
    T = ROW_TILE if seq_len is None else PROMPT_ROW_TILE
    if seq_len is None:
        tiles_per_seq = None
        v_shape = (N, D_C)
        v_spec = pl.BlockSpec((T, D_C), lambda i: (i, 0))
    else:
        tiles_per_seq = seq_len // T
        v_shape = (1, N // seq_len, CHUNK_C, D_C)
        v_spec = pl.BlockSpec((1, 1, CHUNK_C, D_C), lambda i: (0, i // tiles_per_seq, 0, 0))
    kern = functools.partial(_l1_kernel, T=T, R=R, tiles_per_seq=tiles_per_seq)
    return pl.pallas_call(
        kern,
        grid=(N // T,),
        in_specs=[
            pl.BlockSpec((T, D_MODEL), lambda i: (i, 0)),
            _const_spec((1, D_MODEL)),
            _const_spec((D_MODEL, 3 * D_C)),
            _const_spec((1, D_C)),
            _const_spec((1, D_C)),
            _const_spec(wmix.shape),
            _const_spec((H_C, CHUNK_C, 1)),
            _const_spec((D_C, D_MODEL)),
            _const_spec((1, D_MODEL)),
        ],
        out_specs=[pl.BlockSpec((T, D_MODEL), lambda i: (i, 0)), v_spec],
        out_shape=[jax.ShapeDtypeStruct((N, D_MODEL), _F32), jax.ShapeDtypeStruct(v_shape, _F32)],
        scratch_shapes=[
            pltpu.VMEM((T, D_MODEL), _BF16),
            pltpu.VMEM((T, D_C), _BF16),
            pltpu.VMEM((T, D_C), _F32),
            pltpu.VMEM((T, D_C), _BF16),
        ],
        compiler_params=pltpu.CompilerParams(
            dimension_semantics=("arbitrary",),
            vmem_limit_bytes=VMEM_LIMIT_BYTES),
        name="layer1_prompt" if seq_len is not None else "layer1_sample",
    )(xr, gn, win, lng, lnb, wmix, bmix, wout, fn)


def kernel(x_prompt, x_sample, state_conv, state_hgrn, norm_ab, w_in_ab, conv_w, conv_b, ln_a_g, ln_a_b, lb_logits, onorm_b, w_out_ab, norm_c, w_in_c, ln_c_g, ln_c_b, w_s, b_s, w_out_c, final_norm):
    B, L, _ = x_prompt.shape
    NB, LS, _ = x_sample.shape
    row = lambda p: p.reshape(1, -1)

    gn0 = row(norm_ab[0])
    win0 = w_in_ab[0].astype(_BF16)
    wout0 = w_out_ab[0].astype(_BF16)
    cw = jnp.pad(conv_w[0], ((0, HIST_ROWS - CONV_W), (0, 0)))
    cb, lng, lnb, og = row(conv_b[0]), row(ln_a_g[0]), row(ln_a_b[0]), row(onorm_b[0])
    l0_params = (gn0, win0, cw, cb, lng, lnb, lb_logits, og, wout0)

    xp1, conv_prompt, hgrn_prompt = _layer0_prompt(x_prompt, *l0_params)
    xs1, conv_sample, hgrn_sample = _layer0_sample(x_sample, state_conv, state_hgrn, *l0_params)

    gn1 = row(norm_c[0])
    win1 = w_in_c[0].astype(_BF16)
    wout1 = w_out_c[0].astype(_BF16)
    lcg, lcb, fn = row(ln_c_g[0]), row(ln_c_b[0]), row(final_norm)
    reps = CHUNK_C // LS
    wmix_p, bmix_p = w_s[0], b_s[0][:, :, None]
    wmix_s = w_s[0][:, :LS, :]
    bmix_s = jnp.tile(b_s[0][:, :LS], (1, reps))[:, :, None]

    y_prompt, v_prompt = _layer1(xp1.reshape(B * L, D_MODEL), gn1, win1, lcg, lcb, wmix_p, bmix_p, wout1, fn,
                                 R=CHUNK_C, seq_len=L)
    y_sample, v_sample = _layer1(xs1.reshape(NB * LS, D_MODEL), gn1, win1, lcg, lcb, wmix_s, bmix_s, wout1, fn,
                                 R=LS, seq_len=None)

    return (y_prompt.reshape(B, L, D_MODEL), y_sample.reshape(NB, LS, D_MODEL),
            conv_prompt, hgrn_prompt, v_prompt,
            conv_sample, hgrn_sample, v_sample.reshape(1, NB, LS, D_C))
```

```python
import functools

import jax
import jax.numpy as jnp
from jax import lax
from jax.experimental import pallas as pl
from jax.experimental.pallas import tpu as pltpu

D_MODEL = 1024
D_A = D_MODEL
CONV_W = 31
CONV_HIST = CONV_W - 1
H_B = 8
DK_B = 128
DV_B = 128
D_B = H_B * DV_B
HK_B = H_B * DK_B
CHUNK_B = 64
D_C = 2 * D_MODEL
H_C = 8
DG_C = D_C // H_C
CHUNK_C = 128
EPS = 1e-6

SUBLANES = 8
LANES = 128
HIST_ROWS = 32
ROW_TILE = 256
PROMPT_ROW_TILE = 512
VMEM_LIMIT_BYTES = 60 * 1024 * 1024

_F32 = jnp.float32
_BF16 = jnp.bfloat16

_NT = (((1,), (1,)), ((), ()))
_TN = (((0,), (0,)), ((), ()))


def _dot(a, b):
    return jnp.dot(a, b, preferred_element_type=_F32)


def _dot_nt(a, b):
    return lax.dot_general(a, b, _NT, preferred_element_type=_F32)


def _dot_tn(a, b):
    return lax.dot_general(a, b, _TN, preferred_element_type=_F32)


def _rms(x, g):
    return x * lax.rsqrt(jnp.mean(x * x, axis=-1, keepdims=True) + EPS) * g


def _layernorm(x, g, b):
    xc = x - jnp.mean(x, axis=-1, keepdims=True)
    return xc * lax.rsqrt(jnp.mean(xc * xc, axis=-1, keepdims=True) + EPS) * g + b


def _silu(x):
    hx = 0.5 * x
    return hx * jnp.tanh(hx) + hx


_GELU_C = 0.7978845608028654
_GELU_C3 = _GELU_C * 0.044715


def _gelu(x):
    hx = 0.5 * x
    return hx * jnp.tanh(x * (_GELU_C3 * (x * x) + _GELU_C)) + hx


def _split3(x):
    hi = x.astype(_BF16)
    r1 = x - hi.astype(_F32)
    mid = r1.astype(_BF16)
    lo = (r1 - mid.astype(_F32)).astype(_BF16)
    return hi, mid, lo


def _dot_exact_lhs(m, x):
    hi, mid, lo = _split3(x)
    return _dot(m, hi) + _dot(m, mid) + _dot(m, lo)


def _forget_lower_bound(lbl_ref):
    l = lbl_ref[...]
    e = jnp.exp(l - jnp.max(l, axis=0, keepdims=True))
    return e[0:1] / jnp.sum(e, axis=0, keepdims=True)


def _l0_prompt_kernel(x_ref, gn_ref, win_ref, cw_ref, cb_ref, lng_ref, lnb_ref, lbl_ref, og_ref, wout_ref,
                      y_ref, conv_ref, hgrn_ref,
                      abuf, st_s, xn_s, yc_s, z_s, lf_s, k_s, v_s, cat_s, *, T):
    t = pl.program_id(1)
    nt = pl.num_programs(1)
    CG = 256
    RBLK = 256

    @pl.when(t == 0)
    def _():
        abuf[0:HIST_ROWS] = jnp.zeros((HIST_ROWS, D_A), _F32)
        st_s[...] = jnp.zeros_like(st_s)

    off0 = HIST_ROWS - CONV_HIST
    RB = 64
    WIN = RB + HIST_ROWS
    n_lb = D_A // LANES
    z0 = 2 * D_A
    n_dot = (win_ref.shape[1] - z0) // CG
    ZG, ZQ, ZF, ZI, ZB = 0, D_A, D_A + HK_B, D_A + 2 * HK_B, D_A + 2 * HK_B + D_B

    zero_of = lambda v: jnp.minimum(jnp.abs(v), 0.0)

    def conv_block(rb, lb, dep):
        ls = slice(lb * LANES, (lb + 1) * LANES)
        win = abuf[rb * RB:rb * RB + WIN, ls]
        acc = jnp.concatenate([dep, jnp.zeros((RB - SUBLANES, LANES), _F32)], axis=0)
        for s in range(SUBLANES):
            rolled = win if s == 0 else pltpu.roll(win, WIN - s, 0)
            for k in range(CONV_W):
                if (off0 + k) % SUBLANES != s:
                    continue
                base = off0 + k - s
                acc = acc + cw_ref[k:k + 1, ls] * rolled[base:base + RB]
        yc_s[rb * RB:(rb + 1) * RB, ls] = acc
        return zero_of(acc[0:SUBLANES])

    def front(r, dep):
        rs = slice(r * RBLK, (r + 1) * RBLK)
        xn_s[rs] = _rms(x_ref[0, rs], gn_ref[...]).astype(_BF16)
        zdot = lambda c0: _dot(xn_s[rs], win_ref[:, c0:c0 + CG])
        for g in range(D_A // CG):
            a_val = zdot(g * CG)
            a_glu = zdot(D_A + g * CG)
            abuf[HIST_ROWS + r * RBLK:HIST_ROWS + (r + 1) * RBLK, g * CG:(g + 1) * CG] = (
                a_val * jax.nn.sigmoid(a_glu))
        n_conv = (RBLK // RB) * n_lb
        emitted = 0
        dot_deps = {}
        for i in range(n_conv):
            if i - 2 in dot_deps:
                dep = dep + dot_deps.pop(i - 2)
            dep = conv_block(r * (RBLK // RB) + i // n_lb, i % n_lb, dep)
            while emitted < ((i + 1) * n_dot) // n_conv:
                z = zdot(z0 + emitted * CG)
                z_s[rs, emitted * CG:(emitted + 1) * CG] = z
                dot_deps[i] = zero_of(z[0:SUBLANES, 0:LANES])
                emitted += 1
        return dep

    C = CHUNK_B
    ri = lax.broadcasted_iota(jnp.int32, (C, C), 0)
    ci = lax.broadcasted_iota(jnp.int32, (C, C), 1)
    tril = (ci <= ri).astype(_BF16)
    rg = lax.broadcasted_iota(jnp.int32, (RBLK, RBLK), 0)
    cg = lax.broadcasted_iota(jnp.int32, (RBLK, RBLK), 1)
    intra = ((rg ^ cg) < C) & (cg <= rg)
    heads = [slice(h * DK_B, (h + 1) * DK_B) for h in range(H_B)]
    lb = _forget_lower_bound(lbl_ref)

    def back(r):
        gs = slice(r * RBLK, (r + 1) * RBLK)
        yc = _layernorm(yc_s[gs] + cb_ref[...], lng_ref[...], lnb_ref[...])
        cat_a = (_silu(yc) * _silu(z_s[gs, ZG:ZG + D_A])).astype(_BF16)
        y_ref[0, gs] = x_ref[0, gs] + _dot(cat_a, wout_ref[0:D_A, :])

        f = lb + (1.0 - lb) * jax.nn.sigmoid(z_s[gs, ZF:ZF + HK_B])
        lf_s[gs] = jnp.log(f)
        k_s[gs] = 1.0 - f
        v_s[gs] = _silu(z_s[gs, ZI:ZI + D_B]).astype(_BF16)

        qa_l, kb_l, qe_l, kc_l, decay_l = [], [], [], [], []
        for c in range(RBLK // C):
            rs = slice(r * RBLK + c * C, r * RBLK + (c + 1) * C)
            bcum = _dot_exact_lhs(tril, lf_s[rs])
            bmid = bcum[C // 2:C // 2 + 1]
            bend = bcum[C - 1:C]
            d = bcum - bmid
            qa = z_s[rs, ZQ:ZQ + HK_B] * jnp.exp(d)
            kb = k_s[rs] * jnp.exp(-d)
            qe_l.append((qa * jnp.exp(bmid)).astype(_BF16))
            kc_l.append((kb * jnp.exp(bend - bmid)).astype(_BF16))
            qa_l.append(qa.astype(_BF16))
            kb_l.append(kb.astype(_BF16))
            decay_l.append(jnp.exp(bend))
        qa = jnp.concatenate(qa_l, axis=0)
        kb = jnp.concatenate(kb_l, axis=0)
        o_intra = []
        for hs in heads:
            sc = jnp.where(intra, _dot_nt(qa[:, hs], kb[:, hs]), 0.0).astype(_BF16)
            o_intra.append(_dot(sc, v_s[gs, hs]))
        for c in range(RBLK // C):
            rs = slice(r * RBLK + c * C, r * RBLK + (c + 1) * C)
            for h, hs in enumerate(heads):
                st = st_s[h]
                o = o_intra[h][c * C:(c + 1) * C] + _dot_nt(qe_l[c][:, hs], st.astype(_BF16))
                st_s[h] = st * decay_l[c][:, hs] + _dot_tn(v_s[rs, hs], kc_l[c][:, hs])
                gate = _silu(z_s[rs, ZB + h * DV_B:ZB + (h + 1) * DV_B])
                cat_s[rs, hs] = (_rms(o, og_ref[:, hs]) * gate).astype(_BF16)
        y_ref[0, gs] += _dot(cat_s[gs], wout_ref[D_A:D_A + D_B, :])

    dep = jnp.zeros((SUBLANES, LANES), _F32)
    dep = front(0, dep)
    for r in range(T // RBLK):
        if r + 1 < T // RBLK:
            dep = front(r + 1, dep)
        back(r)

    abuf[0:HIST_ROWS] = abuf[T:T + HIST_ROWS]

    @pl.when(t == nt - 1)
    def _():
        conv_ref[0, 0] = abuf[T + off0:T + HIST_ROWS]
        for h in range(H_B):
            hgrn_ref[0, 0, h] = st_s[h].T


def _const_spec(shape):
    nd = len(shape)
    return pl.BlockSpec(shape, lambda *_: (0,) * nd, pipeline_mode=pl.Buffered(1))


def _layer0_prompt(x, gn, win, cw, cb, lng, lnb, lbl, og, wout):
    B, L, _ = x.shape
    T = PROMPT_ROW_TILE
    d_in = win.shape[1]
    kern = functools.partial(_l0_prompt_kernel, T=T)
    return pl.pallas_call(
        kern,
        grid=(B, L // T),
        in_specs=[
            pl.BlockSpec((1, T, D_MODEL), lambda b, t: (b, t, 0)),
            _const_spec((1, D_MODEL)),
            _const_spec((D_MODEL, d_in)),
            _const_spec((HIST_ROWS, D_A)),
            _const_spec((1, D_A)),
            _const_spec((1, D_A)),
            _const_spec((1, D_A)),
            _const_spec(lbl.shape),
            _const_spec((1, D_B)),
            _const_spec((D_A + D_B, D_MODEL)),
        ],
        out_specs=[
            pl.BlockSpec((1, T, D_MODEL), lambda b, t: (b, t, 0)),
            pl.BlockSpec((1, 1, CONV_HIST, D_A), lambda b, t: (0, b, 0, 0)),
            pl.BlockSpec((1, 1, H_B, DK_B, DV_B), lambda b, t: (0, b, 0, 0, 0)),
        ],
        out_shape=[
            jax.ShapeDtypeStruct((B, L, D_MODEL), _F32),
            jax.ShapeDtypeStruct((1, B, CONV_HIST, D_A), _F32),
            jax.ShapeDtypeStruct((1, B, H_B, DK_B, DV_B), _F32),
        ],
        scratch_shapes=[
            pltpu.VMEM((T + HIST_ROWS, D_A), _F32),
            pltpu.VMEM((H_B, DV_B, DK_B), _F32),
            pltpu.VMEM((T, D_MODEL), _BF16),
            pltpu.VMEM((T, D_A), _F32),
            pltpu.VMEM((T, d_in - 2 * D_A), _F32),
            pltpu.VMEM((T, HK_B), _F32),
            pltpu.VMEM((T, HK_B), _F32),
            pltpu.VMEM((T, D_B), _BF16),
            pltpu.VMEM((T, D_B), _BF16),
        ],
        compiler_params=pltpu.CompilerParams(
            dimension_semantics=("arbitrary", "arbitrary"),
            vmem_limit_bytes=VMEM_LIMIT_BYTES),
        name="layer0_prompt",
    )(x, gn, win, cw, cb, lng, lnb, lbl, og, wout)


SEQ_PER_STEP = 4
XP_ROWS = 40


def _l0_sample_kernel(x_ref, sc_ref, sh_ref, gn_ref, win_ref, cw_ref, cb_ref, lng_ref, lnb_ref, lbl_ref,
                      og_ref, wout_ref,
                      y_ref, conv_ref, hgrn_ref,
                      xn_s, a_s, xp_s, yc_s, qe_s, kc_s, v_s, lf_s, o_s, cat_s, *, T, LS):
    j = pl.program_id(1)
    nj = pl.num_programs(1)
    G = T // LS

    @pl.when(j == 0)
    def _():
        x = x_ref[...]
        xn_s[...] = _rms(x, gn_ref[...]).astype(_BF16)
        a_val = _dot(xn_s[...], win_ref[:, 0:D_A])
        a_glu = _dot(xn_s[...], win_ref[:, D_A:2 * D_A])
        a_s[...] = a_val * jax.nn.sigmoid(a_glu)

        def conv_seq(g, carry):
            r = pl.multiple_of(g * LS, LS)
            xp_s[0:CONV_HIST] = sc_ref[0, g]
            xp_s[CONV_HIST:CONV_HIST + LS] = a_s[pl.ds(r, LS)]
            conv_ref[0, g] = xp_s[LS:LS + CONV_HIST]
            for lb in range(D_A // LANES):
                ls = slice(lb * LANES, (lb + 1) * LANES)
                win = xp_s[:, ls]
                acc = jnp.zeros((LS, LANES), _F32)
                for s in range(SUBLANES):
                    rolled = win if s == 0 else pltpu.roll(win, XP_ROWS - s, 0)
                    for k in range(CONV_W):
                        if k % SUBLANES != s:
                            continue
                        acc = acc + cw_ref[k:k + 1, ls] * rolled[k - s:k - s + LS]
                yc_s[pl.ds(r, LS), ls] = acc
            return carry

        xp_s[CONV_HIST + LS:XP_ROWS] = jnp.zeros((XP_ROWS - CONV_HIST - LS, D_A), _F32)
        lax.fori_loop(0, G, conv_seq, 0)

        yc = _layernorm(yc_s[...] + cb_ref[...], lng_ref[...], lnb_ref[...])
        a_gate = _dot(xn_s[...], win_ref[:, 2 * D_A:3 * D_A])
        cat_s[:, 0:D_A] = (_silu(yc) * _silu(a_gate)).astype(_BF16)

        c0 = 3 * D_A
        q = _dot(xn_s[...], win_ref[:, c0:c0 + HK_B])
        lb = _forget_lower_bound(lbl_ref)
        f = lb + (1.0 - lb) * jax.nn.sigmoid(_dot(xn_s[...], win_ref[:, c0 + HK_B:c0 + 2 * HK_B]))
        lf_s[...] = jnp.log(f)
        kk = 1.0 - f
        v_s[...] = _silu(_dot(xn_s[...], win_ref[:, c0 + 2 * HK_B:c0 + 2 * HK_B + D_B]))

        ri = lax.broadcasted_iota(jnp.int32, (LANES, LANES), 0)
        ci = lax.broadcasted_iota(jnp.int32, (LANES, LANES), 1)
        same = (ri ^ ci) < LS
        causal = same & (ci <= ri)
        m_cum = causal.astype(_BF16)
        m_mid = (same & ((ci & (LS - 1)) <= LS // 2)).astype(_BF16)
        m_end = same.astype(_BF16)
        for gr in range(T // LANES):
            rs = slice(gr * LANES, (gr + 1) * LANES)
            hi, mid, lo = _split3(lf_s[rs])
            ex = lambda m: _dot(m, hi) + _dot(m, mid) + _dot(m, lo)
            bcum, bmid, bend = ex(m_cum), ex(m_mid), ex(m_end)
            d = bcum - bmid
            qa = q[rs] * jnp.exp(d)
            kb = kk[rs] * jnp.exp(-d)
            qe_s[rs] = qa * jnp.exp(bmid)
            kc_s[rs] = kb * jnp.exp(bend - bmid)
            qa = qa.astype(_BF16)
            kb = kb.astype(_BF16)
            for h in range(H_B):
                hs = slice(h * DK_B, (h + 1) * DK_B)
                sc = jnp.where(causal, _dot_nt(qa[:, hs], kb[:, hs]), 0.0)
                o_s[rs, hs] = _dot(sc.astype(_BF16), v_s[rs, hs].astype(_BF16))

    rhs_sel = jnp.concatenate([jnp.zeros((3 * LS, DV_B), _F32), jnp.ones((3 * LS, DV_B), _F32)], axis=1)
    zpad = jnp.zeros((LS, DV_B), _F32)
    for u in range(SEQ_PER_STEP):
        r = pl.multiple_of((j * SEQ_PER_STEP + u) * LS, LS)
        qe = qe_s[pl.ds(r, LS)].astype(_BF16)
        vv = v_s[pl.ds(r, LS)]
        hi, mid, lo = _split3(lf_s[pl.ds(r, LS)])
        lhs = jnp.concatenate([kc_s[pl.ds(r, LS)], hi.astype(_F32), mid.astype(_F32), lo.astype(_F32)],
                              axis=0).astype(_BF16)
        for h in range(H_B):
            hs = slice(h * DK_B, (h + 1) * DK_B)
            s0 = sh_ref[0, u, h]
            o_s[pl.ds(r, LS), hs] += _dot(qe[:, hs], s0.astype(_BF16))
            rhs = jnp.concatenate([jnp.concatenate([vv[:, hs], zpad], axis=1), rhs_sel], axis=0).astype(_BF16)
            both = _dot_tn(lhs[:, hs], rhs)
            hgrn_ref[0, u, h] = jnp.exp(both[:, DV_B:]) * s0 + both[:, :DV_B]

    @pl.when(j == nj - 1)
    def _():
        c0 = 3 * D_A + 2 * HK_B + D_B
        b_gate = _dot(xn_s[...], win_ref[:, c0:c0 + D_B])
        for h in range(H_B):
            hs = slice(h * DV_B, (h + 1) * DV_B)
            cat_s[:, D_A + h * DV_B:D_A + (h + 1) * DV_B] = (
                _rms(o_s[:, hs], og_ref[:, hs]) * _silu(b_gate[:, hs])).astype(_BF16)
        y_ref[...] = x_ref[...] + _dot(cat_s[...], wout_ref[...])


def _layer0_sample(x, sconv, shgrn, gn, win, cw, cb, lng, lnb, lbl, og, wout):
    NB, LS, _ = x.shape
    T = ROW_TILE
    G = T // LS
    nsteps = G // SEQ_PER_STEP
    d_in = win.shape[1]
    xr = x.reshape(NB * LS, D_MODEL)
    kern = functools.partial(_l0_sample_kernel, T=T, LS=LS)
    y, conv_new, hgrn_new = pl.pallas_call(
        kern,
        grid=(NB // G, nsteps),
        in_specs=[
            pl.BlockSpec((T, D_MODEL), lambda i, j: (i, 0)),
            pl.BlockSpec((1, G, CONV_HIST, D_A), lambda i, j: (0, i, 0, 0)),
            pl.BlockSpec((1, SEQ_PER_STEP, H_B, DK_B, DV_B), lambda i, j: (0, i * nsteps + j, 0, 0, 0)),
            _const_spec((1, D_MODEL)),
            _const_spec((D_MODEL, d_in)),
            _const_spec((HIST_ROWS, D_A)),
            _const_spec((1, D_A)),
            _const_spec((1, D_A)),
            _const_spec((1, D_A)),
            _const_spec(lbl.shape),
            _const_spec((1, D_B)),
            _const_spec((D_A + D_B, D_MODEL)),
        ],
        out_specs=[
            pl.BlockSpec((T, D_MODEL), lambda i, j: (i, 0)),
            pl.BlockSpec((1, G, CONV_HIST, D_A), lambda i, j: (0, i, 0, 0)),
            pl.BlockSpec((1, SEQ_PER_STEP, H_B, DK_B, DV_B), lambda i, j: (0, i * nsteps + j, 0, 0, 0)),
        ],
        out_shape=[
            jax.ShapeDtypeStruct((NB * LS, D_MODEL), _F32),
            jax.ShapeDtypeStruct(sconv.shape, _F32),
            jax.ShapeDtypeStruct(shgrn.shape, _F32),
        ],
        scratch_shapes=[
            pltpu.VMEM((T, D_MODEL), _BF16),
            pltpu.VMEM((T, D_A), _F32),
            pltpu.VMEM((XP_ROWS, D_A), _F32),
            pltpu.VMEM((T, D_A), _F32),
            pltpu.VMEM((T, HK_B), _F32),
            pltpu.VMEM((T, HK_B), _F32),
            pltpu.VMEM((T, D_B), _F32),
            pltpu.VMEM((T, HK_B), _F32),
            pltpu.VMEM((T, D_B), _F32),
            pltpu.VMEM((T, D_A + D_B), _BF16),
        ],
        compiler_params=pltpu.CompilerParams(
            dimension_semantics=("arbitrary", "arbitrary"),
            vmem_limit_bytes=VMEM_LIMIT_BYTES),
        name="layer0_sample",
    )(xr, sconv, shgrn, gn, win, cw, cb, lng, lnb, lbl, og, wout)
    return y.reshape(NB, LS, D_MODEL), conv_new, hgrn_new


def _l1_kernel(x_ref, gn_ref, win_ref, lng_ref, lnb_ref, wmix_ref, bmix_ref, wout_ref, fn_ref,
               y_ref, v_ref, xn_s, vb_s, ug_s, p_s, *rest, T, R, tiles_per_seq):
    i = pl.program_id(0)
    x = x_ref[...]
    xn_s[...] = _rms(x, gn_ref[...]).astype(_BF16)
    u = _gelu(_dot(xn_s[...], win_ref[:, 0:D_C]))
    gate = _dot(xn_s[...], win_ref[:, 2 * D_C:3 * D_C])
    ug_s[...] = u * _silu(gate)
    v = _layernorm(_gelu(_dot(xn_s[...], win_ref[:, D_C:2 * D_C])), lng_ref[...], lnb_ref[...])
    vb_s[...] = v.astype(_BF16)
    if tiles_per_seq is None:
        v_ref[...] = v
    else:
        vkeep_s, = rest
        vkeep_s[...] = v[T - CHUNK_C:T]

    ri = lax.broadcasted_iota(jnp.int32, (CHUNK_C, CHUNK_C), 0)
    ci = lax.broadcasted_iota(jnp.int32, (CHUNK_C, CHUNK_C), 1)
    mask = ((ri ^ ci) < R) & (ci <= ri)
    for h in range(H_C):
        hs = slice(h * DG_C, (h + 1) * DG_C)
        wg = wmix_ref[h]
        if R < CHUNK_C:
            lane = lax.broadcasted_iota(jnp.int32, wg.shape, 1)
            wg = jnp.where(lane < R, wg, 0.0)
            span = R
            while span < CHUNK_C:
                wg = wg + pltpu.roll(wg, span, 1)
                span *= 2
            wg = jnp.concatenate([wg] * (CHUNK_C // R), axis=0)
        wm = jnp.where(mask, wg, 0.0).astype(_BF16)
        for gr in range(T // CHUNK_C):
            rs = slice(gr * CHUNK_C, (gr + 1) * CHUNK_C)
            mix = _dot(wm, vb_s[rs, hs]) + bmix_ref[h]
            p_s[rs, hs] = (ug_s[rs, hs] * mix).astype(_BF16)
    y = x + _dot(p_s[...], wout_ref[...])
    y_ref[...] = _rms(y, fn_ref[...])
    if tiles_per_seq is not None:
        @pl.when(i % tiles_per_seq == tiles_per_seq - 1)
        def _():
            v_ref[0, 0] = vkeep_s[...]


def _layer1(xr, gn, win, lng, lnb, wmix, bmix, wout, fn, *, R, seq_len):
    N = xr.shape[0]
    T = ROW_TILE if seq_len is None else PROMPT_ROW_TILE
    if seq_len is None:
        tiles_per_seq = None
        v_shape = (N, D_C)
        v_spec = pl.BlockSpec((T, D_C), lambda i: (i, 0))
    else:
        tiles_per_seq = seq_len // T
        v_shape = (1, N // seq_len, CHUNK_C, D_C)
        v_spec = pl.BlockSpec((1, 1, CHUNK_C, D_C), lambda i: (0, i // tiles_per_seq, 0, 0))
    kern = functools.partial(_l1_kernel, T=T, R=R, tiles_per_seq=tiles_per_seq)
    return pl.pallas_call(
        kern,
        grid=(N // T,),
        in_specs=[
            pl.BlockSpec((T, D_MODEL), lambda i: (i, 0)),
            _const_spec((1, D_MODEL)),
            _const_spec((D_MODEL, 3 * D_C)),
            _const_spec((1, D_C)),
            _const_spec((1, D_C)),
            _const_spec(wmix.shape),
            _const_spec((H_C, CHUNK_C, 1)),
            _const_spec((D_C, D_MODEL)),
            _const_spec((1, D_MODEL)),
        ],
        out_specs=[pl.BlockSpec((T, D_MODEL), lambda i: (i, 0)), v_spec],
        out_shape=[jax.ShapeDtypeStruct((N, D_MODEL), _F32), jax.ShapeDtypeStruct(v_shape, _F32)],
        scratch_shapes=[
            pltpu.VMEM((T, D_MODEL), _BF16),
            pltpu.VMEM((T, D_C), _BF16),
            pltpu.VMEM((T, D_C), _F32),
            pltpu.VMEM((T, D_C), _BF16),
        ] + ([] if seq_len is None else [pltpu.VMEM((CHUNK_C, D_C), _F32)]),
        compiler_params=pltpu.CompilerParams(
            dimension_semantics=("arbitrary",),
            vmem_limit_bytes=VMEM_LIMIT_BYTES),
        name="layer1_prompt" if seq_len is not None else "layer1_sample",
    )(xr, gn, win, lng, lnb, wmix, bmix, wout, fn)


def kernel(x_prompt, x_sample, state_conv, state_hgrn, norm_ab, w_in_ab, conv_w, conv_b, ln_a_g, ln_a_b, lb_logits, onorm_b, w_out_ab, norm_c, w_in_c, ln_c_g, ln_c_b, w_s, b_s, w_out_c, final_norm):
    B, L, _ = x_prompt.shape
    NB, LS, _ = x_sample.shape
    row = lambda p: p.reshape(1, -1)

    gn0 = row(norm_ab[0])
    win0 = w_in_ab[0].astype(_BF16)
    wout0 = w_out_ab[0].astype(_BF16)
    cw = jnp.pad(conv_w[0], ((0, HIST_ROWS - CONV_W), (0, 0)))
    cb, lng, lnb, og = row(conv_b[0]), row(ln_a_g[0]), row(ln_a_b[0]), row(onorm_b[0])
    l0_params = (gn0, win0, cw, cb, lng, lnb, lb_logits, og, wout0)

    xp1, conv_prompt, hgrn_prompt = _layer0_prompt(x_prompt, *l0_params)
    xs1, conv_sample, hgrn_sample = _layer0_sample(x_sample, state_conv, state_hgrn, *l0_params)

    gn1 = row(norm_c[0])
    win1 = w_in_c[0].astype(_BF16)
    wout1 = w_out_c[0].astype(_BF16)
    lcg, lcb, fn = row(ln_c_g[0]), row(ln_c_b[0]), row(final_norm)
    reps = CHUNK_C // LS
    wmix_p, bmix_p = w_s[0], b_s[0][:, :, None]
    wmix_s = w_s[0][:, :LS, :]
    bmix_s = jnp.tile(b_s[0][:, :LS], (1, reps))[:, :, None]

    y_prompt, v_prompt = _layer1(xp1.reshape(B * L, D_MODEL), gn1, win1, lcg, lcb, wmix_p, bmix_p, wout1, fn,
                                 R=CHUNK_C, seq_len=L)
    y_sample, v_sample = _layer1(xs1.reshape(NB * LS, D_MODEL), gn1, win1, lcg, lcb, wmix_s, bmix_s, wout1, fn,
                                 R=LS, seq_len=None)

    return (y_prompt.reshape(B, L, D_MODEL), y_sample.reshape(NB, LS, D_MODEL),
            conv_prompt, hgrn_prompt, v_prompt,
            conv_sample, hgrn_sample, v_sample.reshape(1, NB, LS, D_C))
```

```python
import functools

import jax
import jax.numpy as jnp
from jax import lax
from jax.experimental import pallas as pl
from jax.experimental.pallas import tpu as pltpu

D_MODEL = 1024
D_A = D_MODEL
CONV_W = 31
CONV_HIST = CONV_W - 1
H_B = 8
DK_B = 128
DV_B = 128
D_B = H_B * DV_B
HK_B = H_B * DK_B
CHUNK_B = 64
D_C = 2 * D_MODEL
H_C = 8
DG_C = D_C // H_C
CHUNK_C = 128
EPS = 1e-6

SUBLANES = 8
LANES = 128
HIST_ROWS = 32
ROW_TILE = 256
PROMPT_ROW_TILE = 512
VMEM_LIMIT_BYTES = 60 * 1024 * 1024

_F32 = jnp.float32
_BF16 = jnp.bfloat16

_NT = (((1,), (1,)), ((), ()))
_TN = (((0,), (0,)), ((), ()))


def _dot(a, b):
    return jnp.dot(a, b, preferred_element_type=_F32)


def _dot_nt(a, b):
    return lax.dot_general(a, b, _NT, preferred_element_type=_F32)


def _dot_tn(a, b):
    return lax.dot_general(a, b, _TN, preferred_element_type=_F32)


def _rms(x, g):
    return x * lax.rsqrt(jnp.mean(x * x, axis=-1, keepdims=True) + EPS) * g


def _layernorm(x, g, b):
    xc = x - jnp.mean(x, axis=-1, keepdims=True)
    return xc * lax.rsqrt(jnp.mean(xc * xc, axis=-1, keepdims=True) + EPS) * g + b


def _silu(x):
    hx = 0.5 * x
    return hx * jnp.tanh(hx) + hx


_GELU_C = 0.7978845608028654
_GELU_C3 = _GELU_C * 0.044715


def _gelu(x):
    hx = 0.5 * x
    return hx * jnp.tanh(x * (_GELU_C3 * (x * x) + _GELU_C)) + hx


def _split3(x):
    hi = x.astype(_BF16)
    r1 = x - hi.astype(_F32)
    mid = r1.astype(_BF16)
    lo = (r1 - mid.astype(_F32)).astype(_BF16)
    return hi, mid, lo


def _dot_exact_lhs(m, x):
    hi, mid, lo = _split3(x)
    return _dot(m, hi) + _dot(m, mid) + _dot(m, lo)


def _forget_lower_bound(lbl_ref):
    l = lbl_ref[...]
    e = jnp.exp(l - jnp.max(l, axis=0, keepdims=True))
    return e[0:1] / jnp.sum(e, axis=0, keepdims=True)


def _l0_prompt_kernel(x_ref, gn_ref, win_ref, cw_ref, cb_ref, lng_ref, lnb_ref, lbl_ref, og_ref, wout_ref,
                      y_ref, conv_ref, hgrn_ref,
                      abuf, st_s, xn_s, yc_s, z_s, lf_s, k_s, v_s, cat_s, *, T):
    t = pl.program_id(1)
    nt = pl.num_programs(1)
    CG = 256
    RBLK = 256

    @pl.when(t == 0)
    def _():
        abuf[0:HIST_ROWS] = jnp.zeros((HIST_ROWS, D_A), _F32)
        st_s[...] = jnp.zeros_like(st_s)

    off0 = HIST_ROWS - CONV_HIST
    RB = 64
    WIN = RB + HIST_ROWS
    n_lb = D_A // LANES
    z0 = 2 * D_A
    n_dot = (win_ref.shape[1] - z0) // CG
    ZG, ZQ, ZF, ZI, ZB = 0, D_A, D_A + HK_B, D_A + 2 * HK_B, D_A + 2 * HK_B + D_B

    zero_of = lambda v: jnp.minimum(jnp.abs(v), 0.0)

    def conv_block(rb, lb, dep):
        ls = slice(lb * LANES, (lb + 1) * LANES)
        win = abuf[rb * RB:rb * RB + WIN, ls]
        acc = jnp.concatenate([dep, jnp.zeros((RB - SUBLANES, LANES), _F32)], axis=0)
        for s in range(SUBLANES):
            rolled = win if s == 0 else pltpu.roll(win, WIN - s, 0)
            for k in range(CONV_W):
                if (off0 + k) % SUBLANES != s:
                    continue
                base = off0 + k - s
                acc = acc + cw_ref[k:k + 1, ls] * rolled[base:base + RB]
        yc_s[rb * RB:(rb + 1) * RB, ls] = acc
        return zero_of(acc[0:SUBLANES])

    def front(r, dep):
        rs = slice(r * RBLK, (r + 1) * RBLK)
        xn_s[rs] = _rms(x_ref[0, rs], gn_ref[...]).astype(_BF16)
        zdot = lambda c0: _dot(xn_s[rs], win_ref[:, c0:c0 + CG])
        for g in range(D_A // CG):
            a_val = zdot(g * CG)
            a_glu = zdot(D_A + g * CG)
            abuf[HIST_ROWS + r * RBLK:HIST_ROWS + (r + 1) * RBLK, g * CG:(g + 1) * CG] = (
                a_val * jax.nn.sigmoid(a_glu))
        n_conv = (RBLK // RB) * n_lb
        emitted = 0
        dot_deps = {}
        for i in range(n_conv):
            if i - 2 in dot_deps:
                dep = dep + dot_deps.pop(i - 2)
            dep = conv_block(r * (RBLK // RB) + i // n_lb, i % n_lb, dep)
            while emitted < ((i + 1) * n_dot) // n_conv:
                z = zdot(z0 + emitted * CG)
                z_s[rs, emitted * CG:(emitted + 1) * CG] = z
                dot_deps[i] = zero_of(z[0:SUBLANES, 0:LANES])
                emitted += 1
        return dep

    C = CHUNK_B
    ri = lax.broadcasted_iota(jnp.int32, (C, C), 0)
    ci = lax.broadcasted_iota(jnp.int32, (C, C), 1)
    tril = (ci <= ri).astype(_BF16)
    rg = lax.broadcasted_iota(jnp.int32, (RBLK, RBLK), 0)
    cg = lax.broadcasted_iota(jnp.int32, (RBLK, RBLK), 1)
    intra = ((rg ^ cg) < C) & (cg <= rg)
    heads = [slice(h * DK_B, (h + 1) * DK_B) for h in range(H_B)]
    lb = _forget_lower_bound(lbl_ref)

    def back(r):
        gs = slice(r * RBLK, (r + 1) * RBLK)
        yc = _layernorm(yc_s[gs] + cb_ref[...], lng_ref[...], lnb_ref[...])
        cat_a = (_silu(yc) * _silu(z_s[gs, ZG:ZG + D_A])).astype(_BF16)
        y_ref[0, gs] = x_ref[0, gs] + _dot(cat_a, wout_ref[0:D_A, :])

        f = lb + (1.0 - lb) * jax.nn.sigmoid(z_s[gs, ZF:ZF + HK_B])
        lf_s[gs] = jnp.log(f)
        k_s[gs] = 1.0 - f
        v_s[gs] = _silu(z_s[gs, ZI:ZI + D_B]).astype(_BF16)

        qa_l, kb_l, qe_l, kc_l, decay_l = [], [], [], [], []
        for c in range(RBLK // C):
            rs = slice(r * RBLK + c * C, r * RBLK + (c + 1) * C)
            bcum = _dot_exact_lhs(tril, lf_s[rs])
            bmid = bcum[C // 2:C // 2 + 1]
            bend = bcum[C - 1:C]
            d = bcum - bmid
            qa = z_s[rs, ZQ:ZQ + HK_B] * jnp.exp(d)
            kb = k_s[rs] * jnp.exp(-d)
            qe_l.append((qa * jnp.exp(bmid)).astype(_BF16))
            kc_l.append((kb * jnp.exp(bend - bmid)).astype(_BF16))
            qa_l.append(qa.astype(_BF16))
            kb_l.append(kb.astype(_BF16))
            decay_l.append(jnp.exp(bend))
        qa = jnp.concatenate(qa_l, axis=0)
        kb = jnp.concatenate(kb_l, axis=0)
        o_intra = []
        for hs in heads:
            sc = jnp.where(intra, _dot_nt(qa[:, hs], kb[:, hs]), 0.0).astype(_BF16)
            o_intra.append(_dot(sc, v_s[gs, hs]))
        for c in range(RBLK // C):
            rs = slice(r * RBLK + c * C, r * RBLK + (c + 1) * C)
            for h, hs in enumerate(heads):
                st = st_s[h]
                o = o_intra[h][c * C:(c + 1) * C] + _dot_nt(qe_l[c][:, hs], st.astype(_BF16))
                st_s[h] = st * decay_l[c][:, hs] + _dot_tn(v_s[rs, hs], kc_l[c][:, hs])
                gate = _silu(z_s[rs, ZB + h * DV_B:ZB + (h + 1) * DV_B])
                cat_s[rs, hs] = (_rms(o, og_ref[:, hs]) * gate).astype(_BF16)
        y_ref[0, gs] += _dot(cat_s[gs], wout_ref[D_A:D_A + D_B, :])

    dep = jnp.zeros((SUBLANES, LANES), _F32)
    dep = front(0, dep)
    for r in range(T // RBLK):
        if r + 1 < T // RBLK:
            dep = front(r + 1, dep)
        back(r)

    abuf[0:HIST_ROWS] = abuf[T:T + HIST_ROWS]

    @pl.when(t == nt - 1)
    def _():
        conv_ref[0, 0] = abuf[T + off0:T + HIST_ROWS]
        for h in range(H_B):
            hgrn_ref[0, 0, h] = st_s[h].T


def _const_spec(shape):
    nd = len(shape)
    return pl.BlockSpec(shape, lambda *_: (0,) * nd, pipeline_mode=pl.Buffered(1))


def _layer0_prompt(x, gn, win, cw, cb, lng, lnb, lbl, og, wout):
    B, L, _ = x.shape
    T = PROMPT_ROW_TILE
    d_in = win.shape[1]
    kern = functools.partial(_l0_prompt_kernel, T=T)
    return pl.pallas_call(
        kern,
        grid=(B, L // T),
        in_specs=[
            pl.BlockSpec((1, T, D_MODEL), lambda b, t: (b, t, 0)),
            _const_spec((1, D_MODEL)),
            _const_spec((D_MODEL, d_in)),
            _const_spec((HIST_ROWS, D_A)),
            _const_spec((1, D_A)),
            _const_spec((1, D_A)),
            _const_spec((1, D_A)),
            _const_spec(lbl.shape),
            _const_spec((1, D_B)),
            _const_spec((D_A + D_B, D_MODEL)),
        ],
        out_specs=[
            pl.BlockSpec((1, T, D_MODEL), lambda b, t: (b, t, 0)),
            pl.BlockSpec((1, 1, CONV_HIST, D_A), lambda b, t: (0, b, 0, 0)),
            pl.BlockSpec((1, 1, H_B, DK_B, DV_B), lambda b, t: (0, b, 0, 0, 0)),
        ],
        out_shape=[
            jax.ShapeDtypeStruct((B, L, D_MODEL), _F32),
            jax.ShapeDtypeStruct((1, B, CONV_HIST, D_A), _F32),
            jax.ShapeDtypeStruct((1, B, H_B, DK_B, DV_B), _F32),
        ],
        scratch_shapes=[
            pltpu.VMEM((T + HIST_ROWS, D_A), _F32),
            pltpu.VMEM((H_B, DV_B, DK_B), _F32),
            pltpu.VMEM((T, D_MODEL), _BF16),
            pltpu.VMEM((T, D_A), _F32),
            pltpu.VMEM((T, d_in - 2 * D_A), _F32),
            pltpu.VMEM((T, HK_B), _F32),
            pltpu.VMEM((T, HK_B), _F32),
            pltpu.VMEM((T, D_B), _BF16),
            pltpu.VMEM((T, D_B), _BF16),
        ],
        compiler_params=pltpu.CompilerParams(
            dimension_semantics=("arbitrary", "arbitrary"),
            vmem_limit_bytes=VMEM_LIMIT_BYTES),
        name="layer0_prompt",
    )(x, gn, win, cw, cb, lng, lnb, lbl, og, wout)


SEQ_PER_STEP = 4
XP_ROWS = 40


def _l0_sample_kernel(x_ref, sc_ref, sh_ref, gn_ref, win_ref, cw_ref, cb_ref, lng_ref, lnb_ref, lbl_ref,
                      og_ref, wout_ref,
                      y_ref, conv_ref, hgrn_ref,
                      xn_s, a_s, xp_s, yc_s, qe_s, kc_s, v_s, lf_s, o_s, cat_s, *, T, LS):
    j = pl.program_id(1)
    nj = pl.num_programs(1)
    G = T // LS

    @pl.when(j == 0)
    def _():
        x = x_ref[...]
        xn_s[...] = _rms(x, gn_ref[...]).astype(_BF16)
        a_val = _dot(xn_s[...], win_ref[:, 0:D_A])
        a_glu = _dot(xn_s[...], win_ref[:, D_A:2 * D_A])
        a_s[...] = a_val * jax.nn.sigmoid(a_glu)

        def conv_seq(g, carry):
            r = pl.multiple_of(g * LS, LS)
            xp_s[0:CONV_HIST] = sc_ref[0, g]
            xp_s[CONV_HIST:CONV_HIST + LS] = a_s[pl.ds(r, LS)]
            conv_ref[0, g] = xp_s[LS:LS + CONV_HIST]
            for lb in range(D_A // LANES):
                ls = slice(lb * LANES, (lb + 1) * LANES)
                win = xp_s[:, ls]
                acc = jnp.zeros((LS, LANES), _F32)
                for s in range(SUBLANES):
                    rolled = win if s == 0 else pltpu.roll(win, XP_ROWS - s, 0)
                    for k in range(CONV_W):
                        if k % SUBLANES != s:
                            continue
                        acc = acc + cw_ref[k:k + 1, ls] * rolled[k - s:k - s + LS]
                yc_s[pl.ds(r, LS), ls] = acc
            return carry

        xp_s[CONV_HIST + LS:XP_ROWS] = jnp.zeros((XP_ROWS - CONV_HIST - LS, D_A), _F32)
        lax.fori_loop(0, G, conv_seq, 0)

        yc = _layernorm(yc_s[...] + cb_ref[...], lng_ref[...], lnb_ref[...])
        a_gate = _dot(xn_s[...], win_ref[:, 2 * D_A:3 * D_A])
        cat_s[:, 0:D_A] = (_silu(yc) * _silu(a_gate)).astype(_BF16)

        c0 = 3 * D_A
        q = _dot(xn_s[...], win_ref[:, c0:c0 + HK_B])
        lb = _forget_lower_bound(lbl_ref)
        f = lb + (1.0 - lb) * jax.nn.sigmoid(_dot(xn_s[...], win_ref[:, c0 + HK_B:c0 + 2 * HK_B]))
        lf_s[...] = jnp.log(f)
        kk = 1.0 - f
        v_s[...] = _silu(_dot(xn_s[...], win_ref[:, c0 + 2 * HK_B:c0 + 2 * HK_B + D_B]))

        ri = lax.broadcasted_iota(jnp.int32, (LANES, LANES), 0)
        ci = lax.broadcasted_iota(jnp.int32, (LANES, LANES), 1)
        same = (ri ^ ci) < LS
        causal = same & (ci <= ri)
        m_cum = causal.astype(_BF16)
        m_mid = (same & ((ci & (LS - 1)) <= LS // 2)).astype(_BF16)
        m_end = same.astype(_BF16)
        for gr in range(T // LANES):
            rs = slice(gr * LANES, (gr + 1) * LANES)
            hi, mid, lo = _split3(lf_s[rs])
            ex = lambda m: _dot(m, hi) + _dot(m, mid) + _dot(m, lo)
            bcum, bmid, bend = ex(m_cum), ex(m_mid), ex(m_end)
            d = bcum - bmid
            qa = q[rs] * jnp.exp(d)
            kb = kk[rs] * jnp.exp(-d)
            qe_s[rs] = qa * jnp.exp(bmid)
            kc_s[rs] = kb * jnp.exp(bend - bmid)
            qa = qa.astype(_BF16)
            kb = kb.astype(_BF16)
            for h in range(H_B):
                hs = slice(h * DK_B, (h + 1) * DK_B)
                sc = jnp.where(causal, _dot_nt(qa[:, hs], kb[:, hs]), 0.0)
                o_s[rs, hs] = _dot(sc.astype(_BF16), v_s[rs, hs].astype(_BF16))

    rhs_sel = jnp.concatenate([jnp.zeros((3 * LS, DV_B), _F32), jnp.ones((3 * LS, DV_B), _F32)], axis=1)
    zpad = jnp.zeros((LS, DV_B), _F32)
    for u in range(SEQ_PER_STEP):
        r = pl.multiple_of((j * SEQ_PER_STEP + u) * LS, LS)
        qe = qe_s[pl.ds(r, LS)].astype(_BF16)
        vv = v_s[pl.ds(r, LS)]
        hi, mid, lo = _split3(lf_s[pl.ds(r, LS)])
        lhs = jnp.concatenate([kc_s[pl.ds(r, LS)], hi.astype(_F32), mid.astype(_F32), lo.astype(_F32)],
                              axis=0).astype(_BF16)
        for h in range(H_B):
            hs = slice(h * DK_B, (h + 1) * DK_B)
            s0 = sh_ref[0, u, h]
            o_s[pl.ds(r, LS), hs] += _dot(qe[:, hs], s0.astype(_BF16))
            rhs = jnp.concatenate([jnp.concatenate([vv[:, hs], zpad], axis=1), rhs_sel], axis=0).astype(_BF16)
            both = _dot_tn(lhs[:, hs], rhs)
            hgrn_ref[0, u, h] = jnp.exp(both[:, DV_B:]) * s0 + both[:, :DV_B]

    @pl.when(j == nj - 1)
    def _():
        c0 = 3 * D_A + 2 * HK_B + D_B
        b_gate = _dot(xn_s[...], win_ref[:, c0:c0 + D_B])
        for h in range(H_B):
            hs = slice(h * DV_B, (h + 1) * DV_B)
            cat_s[:, D_A + h * DV_B:D_A + (h + 1) * DV_B] = (
                _rms(o_s[:, hs], og_ref[:, hs]) * _silu(b_gate[:, hs])).astype(_BF16)
        y_ref[...] = x_ref[...] + _dot(cat_s[...], wout_ref[...])


def _layer0_sample(x, sconv, shgrn, gn, win, cw, cb, lng, lnb, lbl, og, wout):
    NB, LS, _ = x.shape
    T = ROW_TILE
    G = T // LS
    nsteps = G // SEQ_PER_STEP
    d_in = win.shape[1]
    xr = x.reshape(NB * LS, D_MODEL)
    kern = functools.partial(_l0_sample_kernel, T=T, LS=LS)
    y, conv_new, hgrn_new = pl.pallas_call(
        kern,
        grid=(NB // G, nsteps),
        in_specs=[
            pl.BlockSpec((T, D_MODEL), lambda i, j: (i, 0)),
            pl.BlockSpec((1, G, CONV_HIST, D_A), lambda i, j: (0, i, 0, 0)),
            pl.BlockSpec((1, SEQ_PER_STEP, H_B, DK_B, DV_B), lambda i, j: (0, i * nsteps + j, 0, 0, 0)),
            _const_spec((1, D_MODEL)),
            _const_spec((D_MODEL, d_in)),
            _const_spec((HIST_ROWS, D_A)),
            _const_spec((1, D_A)),
            _const_spec((1, D_A)),
            _const_spec((1, D_A)),
            _const_spec(lbl.shape),
            _const_spec((1, D_B)),
            _const_spec((D_A + D_B, D_MODEL)),
        ],
        out_specs=[
            pl.BlockSpec((T, D_MODEL), lambda i, j: (i, 0)),
            pl.BlockSpec((1, G, CONV_HIST, D_A), lambda i, j: (0, i, 0, 0)),
            pl.BlockSpec((1, SEQ_PER_STEP, H_B, DK_B, DV_B), lambda i, j: (0, i * nsteps + j, 0, 0, 0)),
        ],
        out_shape=[
            jax.ShapeDtypeStruct((NB * LS, D_MODEL), _F32),
            jax.ShapeDtypeStruct(sconv.shape, _F32),
            jax.ShapeDtypeStruct(shgrn.shape, _F32),
        ],
        scratch_shapes=[
            pltpu.VMEM((T, D_MODEL), _BF16),
            pltpu.VMEM((T, D_A), _F32),
            pltpu.VMEM((XP_ROWS, D_A), _F32),
            pltpu.VMEM((T, D_A), _F32),
            pltpu.VMEM((T, HK_B), _F32),
            pltpu.VMEM((T, HK_B), _F32),
            pltpu.VMEM((T, D_B), _F32),
            pltpu.VMEM((T, HK_B), _F32),
            pltpu.VMEM((T, D_B), _F32),
            pltpu.VMEM((T, D_A + D_B), _BF16),
        ],
        compiler_params=pltpu.CompilerParams(
            dimension_semantics=("arbitrary", "arbitrary"),
            vmem_limit_bytes=VMEM_LIMIT_BYTES),
        name="layer0_sample",
    )(xr, sconv, shgrn, gn, win, cw, cb, lng, lnb, lbl, og, wout)
    return y.reshape(NB, LS, D_MODEL), conv_new, hgrn_new


def _l1_kernel(x_ref, gn_ref, win_ref, lng_ref, lnb_ref, wmix_ref, bmix_ref, wout_ref, fn_ref,
               y_ref, v_ref, xn_s, z_s, vb_s, ug_s, p_s, *rest, T, R, tiles_per_seq):
    i = pl.program_id(0)
    RBLK = min(T, 256)
    nblk = T // RBLK
    zero_of = lambda t: jnp.minimum(jnp.abs(t), 0.0)

    ri = lax.broadcasted_iota(jnp.int32, (CHUNK_C, CHUNK_C), 0)
    ci = lax.broadcasted_iota(jnp.int32, (CHUNK_C, CHUNK_C), 1)
    mask = ((ri ^ ci) < R) & (ci <= ri)
    wms = []
    for h in range(H_C):
        wg = wmix_ref[h]
        if R < CHUNK_C:
            lane = lax.broadcasted_iota(jnp.int32, wg.shape, 1)
            wg = jnp.where(lane < R, wg, 0.0)
            span = R
            while span < CHUNK_C:
                wg = wg + pltpu.roll(wg, span, 1)
                span *= 2
            wg = jnp.concatenate([wg] * (CHUNK_C // R), axis=0)
        wms.append(jnp.where(mask, wg, 0.0).astype(_BF16))

    def front(r):
        rows = slice(r * RBLK, (r + 1) * RBLK)
        xn_s[rows] = _rms(x_ref[rows], gn_ref[...]).astype(_BF16)
        z_s[rows] = _dot(xn_s[rows], win_ref[...])

    def back(r, dep):
        rows = slice(r * RBLK, (r + 1) * RBLK)
        zu = z_s[rows, 0:D_C]
        if dep is not None:
            top = jnp.concatenate([zu[0:SUBLANES, 0:LANES] + dep, zu[0:SUBLANES, LANES:]], axis=1)
            zu = jnp.concatenate([top, zu[SUBLANES:]], axis=0)
        ug_s[rows] = _gelu(zu) * _silu(z_s[rows, 2 * D_C:3 * D_C])
        v = _layernorm(_gelu(z_s[rows, D_C:2 * D_C]), lng_ref[...], lnb_ref[...])
        vb_s[rows] = v.astype(_BF16)
        if tiles_per_seq is None:
            v_ref[rows] = v
        elif r == nblk - 1:
            rest[0][...] = v[RBLK - CHUNK_C:RBLK]
        for h in range(H_C):
            hs = slice(h * DG_C, (h + 1) * DG_C)
            for gr in range(RBLK // CHUNK_C):
                rs = slice(r * RBLK + gr * CHUNK_C, r * RBLK + (gr + 1) * CHUNK_C)
                mix = _dot(wms[h], vb_s[rs, hs]) + bmix_ref[h]
                p_s[rs, hs] = (ug_s[rs, hs] * mix).astype(_BF16)
        y = x_ref[rows] + _dot(p_s[rows], wout_ref[...])
        y_ref[rows] = _rms(y, fn_ref[...])
        return zero_of(y[0:SUBLANES, 0:LANES])

    front(0)
    dep = None
    for r in range(nblk):
        if r + 1 < nblk:
            front(r + 1)
        dep = back(r, dep)

    if tiles_per_seq is not None:
        @pl.when(i % tiles_per_seq == tiles_per_seq - 1)
        def _():
            v_ref[0, 0] = rest[0][...]


def _layer1(xr, gn, win, lng, lnb, wmix, bmix, wout, fn, *, R, seq_len):
    N = xr.shape[0]
    T = ROW_TILE if seq_len is None else PROMPT_ROW_TILE
    if seq_len is None:
        tiles_per_seq = None
        v_shape = (N, D_C)
        v_spec = pl.BlockSpec((T, D_C), lambda i: (i, 0))
    else:
        tiles_per_seq = seq_len // T
        v_shape = (1, N // seq_len, CHUNK_C, D_C)
        v_spec = pl.BlockSpec((1, 1, CHUNK_C, D_C), lambda i: (0, i // tiles_per_seq, 0, 0))
    kern = functools.partial(_l1_kernel, T=T, R=R, tiles_per_seq=tiles_per_seq)
    return pl.pallas_call(
        kern,
        grid=(N // T,),
        in_specs=[
            pl.BlockSpec((T, D_MODEL), lambda i: (i, 0)),
            _const_spec((1, D_MODEL)),
            _const_spec((D_MODEL, 3 * D_C)),
            _const_spec((1, D_C)),
            _const_spec((1, D_C)),
            _const_spec(wmix.shape),
            _const_spec((H_C, CHUNK_C, 1)),
            _const_spec((D_C, D_MODEL)),
            _const_spec((1, D_MODEL)),
        ],
        out_specs=[pl.BlockSpec((T, D_MODEL), lambda i: (i, 0)), v_spec],
        out_shape=[jax.ShapeDtypeStruct((N, D_MODEL), _F32), jax.ShapeDtypeStruct(v_shape, _F32)],
        scratch_shapes=[
            pltpu.VMEM((T, D_MODEL), _BF16),
            pltpu.VMEM((T, 3 * D_C), _F32),
            pltpu.VMEM((T, D_C), _BF16),
            pltpu.VMEM((T, D_C), _F32),
            pltpu.VMEM((T, D_C), _BF16),
        ] + ([] if seq_len is None else [pltpu.VMEM((CHUNK_C, D_C), _F32)]),
        compiler_params=pltpu.CompilerParams(
            dimension_semantics=("arbitrary",),
            vmem_limit_bytes=VMEM_LIMIT_BYTES),
        name="layer1_prompt" if seq_len is not None else "layer1_sample",
    )(xr, gn, win, lng, lnb, wmix, bmix, wout, fn)


def kernel(x_prompt, x_sample, state_conv, state_hgrn, norm_ab, w_in_ab, conv_w, conv_b, ln_a_g, ln_a_b, lb_logits, onorm_b, w_out_ab, norm_c, w_in_c, ln_c_g, ln_c_b, w_s, b_s, w_out_c, final_norm):
    B, L, _ = x_prompt.shape
    NB, LS, _ = x_sample.shape
    row = lambda p: p.reshape(1, -1)

    gn0 = row(norm_ab[0])
    win0 = w_in_ab[0].astype(_BF16)
    wout0 = w_out_ab[0].astype(_BF16)
    cw = jnp.pad(conv_w[0], ((0, HIST_ROWS - CONV_W), (0, 0)))
    cb, lng, lnb, og = row(conv_b[0]), row(ln_a_g[0]), row(ln_a_b[0]), row(onorm_b[0])
    l0_params = (gn0, win0, cw, cb, lng, lnb, lb_logits, og, wout0)

    xp1, conv_prompt, hgrn_prompt = _layer0_prompt(x_prompt, *l0_params)
    xs1, conv_sample, hgrn_sample = _layer0_sample(x_sample, state_conv, state_hgrn, *l0_params)

    gn1 = row(norm_c[0])
    win1 = w_in_c[0].astype(_BF16)
    wout1 = w_out_c[0].astype(_BF16)
    lcg, lcb, fn = row(ln_c_g[0]), row(ln_c_b[0]), row(final_norm)
    reps = CHUNK_C // LS
    wmix_p, bmix_p = w_s[0], b_s[0][:, :, None]
    wmix_s = w_s[0][:, :LS, :]
    bmix_s = jnp.tile(b_s[0][:, :LS], (1, reps))[:, :, None]

    y_prompt, v_prompt = _layer1(xp1.reshape(B * L, D_MODEL), gn1, win1, lcg, lcb, wmix_p, bmix_p, wout1, fn,
                                 R=CHUNK_C, seq_len=L)
    y_sample, v_sample = _layer1(xs1.reshape(NB * LS, D_MODEL), gn1, win1, lcg, lcb, wmix_s, bmix_s, wout1, fn,
                                 R=LS, seq_len=None)

    return (y_prompt.reshape(B, L, D_MODEL), y_sample.reshape(NB, LS, D_MODEL),
            conv_prompt, hgrn_prompt, v_prompt,
            conv_sample, hgrn_sample, v_sample.reshape(1, NB, LS, D_C))
```

```python
import functools

import jax
import jax.numpy as jnp
from jax import lax
from jax.experimental import pallas as pl
from jax.experimental.pallas import tpu as pltpu

D_MODEL = 1024
D_A = D_MODEL
CONV_W = 31
CONV_HIST = CONV_W - 1
H_B = 8
DK_B = 128
DV_B = 128
D_B = H_B * DV_B
HK_B = H_B * DK_B
CHUNK_B = 64
D_C = 2 * D_MODEL
H_C = 8
DG_C = D_C // H_C
CHUNK_C = 128
EPS = 1e-6

SUBLANES = 8
LANES = 128
HIST_ROWS = 32
ROW_TILE = 256
PROMPT_ROW_TILE = 512
L1_PROMPT_ROW_TILE = 512
L1_SAMPLE_ROW_TILE = 512
VMEM_LIMIT_BYTES = 60 * 1024 * 1024

_F32 = jnp.float32
_BF16 = jnp.bfloat16

_NT = (((1,), (1,)), ((), ()))
_TN = (((0,), (0,)), ((), ()))


def _dot(a, b):
    return jnp.dot(a, b, preferred_element_type=_F32)


def _dot_nt(a, b):
    return lax.dot_general(a, b, _NT, preferred_element_type=_F32)


def _dot_tn(a, b):
    return lax.dot_general(a, b, _TN, preferred_element_type=_F32)


def _rms(x, g):
    return x * lax.rsqrt(jnp.mean(x * x, axis=-1, keepdims=True) + EPS) * g


def _layernorm(x, g, b):
    xc = x - jnp.mean(x, axis=-1, keepdims=True)
    return xc * lax.rsqrt(jnp.mean(xc * xc, axis=-1, keepdims=True) + EPS) * g + b


def _silu(x):
    hx = 0.5 * x
    return hx * jnp.tanh(hx) + hx


_GELU_C = 0.7978845608028654
_GELU_C3 = _GELU_C * 0.044715


def _gelu(x):
    hx = 0.5 * x
    return hx * jnp.tanh(x * (_GELU_C3 * (x * x) + _GELU_C)) + hx


def _split3(x):
    hi = x.astype(_BF16)
    r1 = x - hi.astype(_F32)
    mid = r1.astype(_BF16)
    lo = (r1 - mid.astype(_F32)).astype(_BF16)
    return hi, mid, lo


def _dot_exact_lhs(m, x):
    hi, mid, lo = _split3(x)
    return _dot(m, hi) + _dot(m, mid) + _dot(m, lo)


def _forget_lower_bound(lbl_ref):
    l = lbl_ref[...]
    e = jnp.exp(l - jnp.max(l, axis=0, keepdims=True))
    return e[0:1] / jnp.sum(e, axis=0, keepdims=True)


def _l0_prompt_kernel(x_ref, gn_ref, win_ref, cw_ref, cb_ref, lng_ref, lnb_ref, lbl_ref, og_ref, wout_ref,
                      y_ref, conv_ref, hgrn_ref,
                      abuf, st_s, xn_s, yc_s, z_s, lf_s, k_s, v_s, cat_s, *, T):
    t = pl.program_id(1)
    nt = pl.num_programs(1)
    CG = 256
    RBLK = 256

    @pl.when(t == 0)
    def _():
        abuf[0:HIST_ROWS] = jnp.zeros((HIST_ROWS, D_A), _F32)
        st_s[...] = jnp.zeros_like(st_s)

    off0 = HIST_ROWS - CONV_HIST
    RB = 64
    WIN = RB + HIST_ROWS
    n_lb = D_A // LANES
    z0 = 2 * D_A
    n_dot = (win_ref.shape[1] - z0) // CG
    ZG, ZQ, ZF, ZI, ZB = 0, D_A, D_A + HK_B, D_A + 2 * HK_B, D_A + 2 * HK_B + D_B

    zero_of = lambda v: jnp.minimum(jnp.abs(v), 0.0)

    def conv_block(rb, lb, dep):
        ls = slice(lb * LANES, (lb + 1) * LANES)
        win = abuf[rb * RB:rb * RB + WIN, ls]
        acc = jnp.concatenate([dep, jnp.zeros((RB - SUBLANES, LANES), _F32)], axis=0)
        for s in range(SUBLANES):
            rolled = win if s == 0 else pltpu.roll(win, WIN - s, 0)
            for k in range(CONV_W):
                if (off0 + k) % SUBLANES != s:
                    continue
                base = off0 + k - s
                acc = acc + cw_ref[k:k + 1, ls] * rolled[base:base + RB]
        yc_s[rb * RB:(rb + 1) * RB, ls] = acc
        return zero_of(acc[0:SUBLANES])

    def front(r, dep):
        rs = slice(r * RBLK, (r + 1) * RBLK)
        xn_s[rs] = _rms(x_ref[0, rs], gn_ref[...]).astype(_BF16)
        zdot = lambda c0: _dot(xn_s[rs], win_ref[:, c0:c0 + CG])
        for g in range(D_A // CG):
            a_val = zdot(g * CG)
            a_glu = zdot(D_A + g * CG)
            abuf[HIST_ROWS + r * RBLK:HIST_ROWS + (r + 1) * RBLK, g * CG:(g + 1) * CG] = (
                a_val * jax.nn.sigmoid(a_glu))
        n_conv = (RBLK // RB) * n_lb
        emitted = 0
        dot_deps = {}
        for i in range(n_conv):
            if i - 2 in dot_deps:
                dep = dep + dot_deps.pop(i - 2)
            dep = conv_block(r * (RBLK // RB) + i // n_lb, i % n_lb, dep)
            while emitted < ((i + 1) * n_dot) // n_conv:
                z = zdot(z0 + emitted * CG)
                z_s[rs, emitted * CG:(emitted + 1) * CG] = z
                dot_deps[i] = zero_of(z[0:SUBLANES, 0:LANES])
                emitted += 1
        return dep

    C = CHUNK_B
    ri = lax.broadcasted_iota(jnp.int32, (C, C), 0)
    ci = lax.broadcasted_iota(jnp.int32, (C, C), 1)
    tril = (ci <= ri).astype(_BF16)
    rg = lax.broadcasted_iota(jnp.int32, (RBLK, RBLK), 0)
    cg = lax.broadcasted_iota(jnp.int32, (RBLK, RBLK), 1)
    intra = ((rg ^ cg) < C) & (cg <= rg)
    heads = [slice(h * DK_B, (h + 1) * DK_B) for h in range(H_B)]
    lb = _forget_lower_bound(lbl_ref)

    def back(r):
        gs = slice(r * RBLK, (r + 1) * RBLK)
        yc = _layernorm(yc_s[gs] + cb_ref[...], lng_ref[...], lnb_ref[...])
        cat_a = (_silu(yc) * _silu(z_s[gs, ZG:ZG + D_A])).astype(_BF16)
        y_ref[0, gs] = x_ref[0, gs] + _dot(cat_a, wout_ref[0:D_A, :])

        f = lb + (1.0 - lb) * jax.nn.sigmoid(z_s[gs, ZF:ZF + HK_B])
        lf_s[gs] = jnp.log(f)
        k_s[gs] = 1.0 - f
        v_s[gs] = _silu(z_s[gs, ZI:ZI + D_B]).astype(_BF16)

        qa_l, kb_l, qe_l, kc_l, decay_l = [], [], [], [], []
        for c in range(RBLK // C):
            rs = slice(r * RBLK + c * C, r * RBLK + (c + 1) * C)
            bcum = _dot_exact_lhs(tril, lf_s[rs])
            bmid = bcum[C // 2:C // 2 + 1]
            bend = bcum[C - 1:C]
            d = bcum - bmid
            qa = z_s[rs, ZQ:ZQ + HK_B] * jnp.exp(d)
            kb = k_s[rs] * jnp.exp(-d)
            qe_l.append((qa * jnp.exp(bmid)).astype(_BF16))
            kc_l.append((kb * jnp.exp(bend - bmid)).astype(_BF16))
            qa_l.append(qa.astype(_BF16))
            kb_l.append(kb.astype(_BF16))
            decay_l.append(jnp.exp(bend))
        qa = jnp.concatenate(qa_l, axis=0)
        kb = jnp.concatenate(kb_l, axis=0)
        o_intra = []
        for hs in heads:
            sc = jnp.where(intra, _dot_nt(qa[:, hs], kb[:, hs]), 0.0).astype(_BF16)
            o_intra.append(_dot(sc, v_s[gs, hs]))
        for c in range(RBLK // C):
            rs = slice(r * RBLK + c * C, r * RBLK + (c + 1) * C)
            for h, hs in enumerate(heads):
                st = st_s[h]
                o = o_intra[h][c * C:(c + 1) * C] + _dot_nt(qe_l[c][:, hs], st.astype(_BF16))
                st_s[h] = st * decay_l[c][:, hs] + _dot_tn(v_s[rs, hs], kc_l[c][:, hs])
                gate = _silu(z_s[rs, ZB + h * DV_B:ZB + (h + 1) * DV_B])
                cat_s[rs, hs] = (_rms(o, og_ref[:, hs]) * gate).astype(_BF16)
        y_ref[0, gs] += _dot(cat_s[gs], wout_ref[D_A:D_A + D_B, :])

    dep = jnp.zeros((SUBLANES, LANES), _F32)
    dep = front(0, dep)
    for r in range(T // RBLK):
        if r + 1 < T // RBLK:
            dep = front(r + 1, dep)
        back(r)

    abuf[0:HIST_ROWS] = abuf[T:T + HIST_ROWS]

    @pl.when(t == nt - 1)
    def _():
        conv_ref[0, 0] = abuf[T + off0:T + HIST_ROWS]
        for h in range(H_B):
            hgrn_ref[0, 0, h] = st_s[h].T


def _const_spec(shape):
    nd = len(shape)
    return pl.BlockSpec(shape, lambda *_: (0,) * nd, pipeline_mode=pl.Buffered(1))


def _layer0_prompt(x, gn, win, cw, cb, lng, lnb, lbl, og, wout):
    B, L, _ = x.shape
    T = PROMPT_ROW_TILE
    d_in = win.shape[1]
    kern = functools.partial(_l0_prompt_kernel, T=T)
    return pl.pallas_call(
        kern,
        grid=(B, L // T),
        in_specs=[
            pl.BlockSpec((1, T, D_MODEL), lambda b, t: (b, t, 0)),
            _const_spec((1, D_MODEL)),
            _const_spec((D_MODEL, d_in)),
            _const_spec((HIST_ROWS, D_A)),
            _const_spec((1, D_A)),
            _const_spec((1, D_A)),
            _const_spec((1, D_A)),
            _const_spec(lbl.shape),
            _const_spec((1, D_B)),
            _const_spec((D_A + D_B, D_MODEL)),
        ],
        out_specs=[
            pl.BlockSpec((1, T, D_MODEL), lambda b, t: (b, t, 0)),
            pl.BlockSpec((1, 1, CONV_HIST, D_A), lambda b, t: (0, b, 0, 0)),
            pl.BlockSpec((1, 1, H_B, DK_B, DV_B), lambda b, t: (0, b, 0, 0, 0)),
        ],
        out_shape=[
            jax.ShapeDtypeStruct((B, L, D_MODEL), _F32),
            jax.ShapeDtypeStruct((1, B, CONV_HIST, D_A), _F32),
            jax.ShapeDtypeStruct((1, B, H_B, DK_B, DV_B), _F32),
        ],
        scratch_shapes=[
            pltpu.VMEM((T + HIST_ROWS, D_A), _F32),
            pltpu.VMEM((H_B, DV_B, DK_B), _F32),
            pltpu.VMEM((T, D_MODEL), _BF16),
            pltpu.VMEM((T, D_A), _F32),
            pltpu.VMEM((T, d_in - 2 * D_A), _F32),
            pltpu.VMEM((T, HK_B), _F32),
            pltpu.VMEM((T, HK_B), _F32),
            pltpu.VMEM((T, D_B), _BF16),
            pltpu.VMEM((T, D_B), _BF16),
        ],
        compiler_params=pltpu.CompilerParams(
            dimension_semantics=("arbitrary", "arbitrary"),
            vmem_limit_bytes=VMEM_LIMIT_BYTES),
        name="layer0_prompt",
    )(x, gn, win, cw, cb, lng, lnb, lbl, og, wout)


SEQ_PER_STEP = 4
XP_ROWS = 40


def _l0_sample_kernel(x_ref, sc_ref, sh_ref, gn_ref, win_ref, cw_ref, cb_ref, lng_ref, lnb_ref, lbl_ref,
                      og_ref, wout_ref,
                      y_ref, conv_ref, hgrn_ref,
                      xn_s, a_s, xp_s, yc_s, qe_s, kc_s, v_s, lf_s, o_s, cat_s, *, T, LS):
    j = pl.program_id(1)
    nj = pl.num_programs(1)
    G = T // LS

    @pl.when(j == 0)
    def _():
        x = x_ref[...]
        xn_s[...] = _rms(x, gn_ref[...]).astype(_BF16)
        a_val = _dot(xn_s[...], win_ref[:, 0:D_A])
        a_glu = _dot(xn_s[...], win_ref[:, D_A:2 * D_A])
        a_s[...] = a_val * jax.nn.sigmoid(a_glu)

        def conv_seq(g, carry):
            r = pl.multiple_of(g * LS, LS)
            xp_s[0:CONV_HIST] = sc_ref[0, g]
            xp_s[CONV_HIST:CONV_HIST + LS] = a_s[pl.ds(r, LS)]
            conv_ref[0, g] = xp_s[LS:LS + CONV_HIST]
            for lb in range(D_A // LANES):
                ls = slice(lb * LANES, (lb + 1) * LANES)
                win = xp_s[:, ls]
                acc = jnp.zeros((LS, LANES), _F32)
                for s in range(SUBLANES):
                    rolled = win if s == 0 else pltpu.roll(win, XP_ROWS - s, 0)
                    for k in range(CONV_W):
                        if k % SUBLANES != s:
                            continue
                        acc = acc + cw_ref[k:k + 1, ls] * rolled[k - s:k - s + LS]
                yc_s[pl.ds(r, LS), ls] = acc
            return carry

        xp_s[CONV_HIST + LS:XP_ROWS] = jnp.zeros((XP_ROWS - CONV_HIST - LS, D_A), _F32)
        lax.fori_loop(0, G, conv_seq, 0)

        yc = _layernorm(yc_s[...] + cb_ref[...], lng_ref[...], lnb_ref[...])
        a_gate = _dot(xn_s[...], win_ref[:, 2 * D_A:3 * D_A])
        cat_s[:, 0:D_A] = (_silu(yc) * _silu(a_gate)).astype(_BF16)

        c0 = 3 * D_A
        q = _dot(xn_s[...], win_ref[:, c0:c0 + HK_B])
        lb = _forget_lower_bound(lbl_ref)
        f = lb + (1.0 - lb) * jax.nn.sigmoid(_dot(xn_s[...], win_ref[:, c0 + HK_B:c0 + 2 * HK_B]))
        lf_s[...] = jnp.log(f)
        kk = 1.0 - f
        v_s[...] = _silu(_dot(xn_s[...], win_ref[:, c0 + 2 * HK_B:c0 + 2 * HK_B + D_B]))

        ri = lax.broadcasted_iota(jnp.int32, (LANES, LANES), 0)
        ci = lax.broadcasted_iota(jnp.int32, (LANES, LANES), 1)
        same = (ri ^ ci) < LS
        causal = same & (ci <= ri)
        m_cum = causal.astype(_BF16)
        m_mid = (same & ((ci & (LS - 1)) <= LS // 2)).astype(_BF16)
        m_end = same.astype(_BF16)
        for gr in range(T // LANES):
            rs = slice(gr * LANES, (gr + 1) * LANES)
            hi, mid, lo = _split3(lf_s[rs])
            ex = lambda m: _dot(m, hi) + _dot(m, mid) + _dot(m, lo)
            bcum, bmid, bend = ex(m_cum), ex(m_mid), ex(m_end)
            d = bcum - bmid
            qa = q[rs] * jnp.exp(d)
            kb = kk[rs] * jnp.exp(-d)
            qe_s[rs] = qa * jnp.exp(bmid)
            kc_s[rs] = kb * jnp.exp(bend - bmid)
            qa = qa.astype(_BF16)
            kb = kb.astype(_BF16)
            for h in range(H_B):
                hs = slice(h * DK_B, (h + 1) * DK_B)
                sc = jnp.where(causal, _dot_nt(qa[:, hs], kb[:, hs]), 0.0)
                o_s[rs, hs] = _dot(sc.astype(_BF16), v_s[rs, hs].astype(_BF16))

    rhs_sel = jnp.concatenate([jnp.zeros((3 * LS, DV_B), _F32), jnp.ones((3 * LS, DV_B), _F32)], axis=1)
    zpad = jnp.zeros((LS, DV_B), _F32)
    for u in range(SEQ_PER_STEP):
        r = pl.multiple_of((j * SEQ_PER_STEP + u) * LS, LS)
        qe = qe_s[pl.ds(r, LS)].astype(_BF16)
        vv = v_s[pl.ds(r, LS)]
        hi, mid, lo = _split3(lf_s[pl.ds(r, LS)])
        lhs = jnp.concatenate([kc_s[pl.ds(r, LS)], hi.astype(_F32), mid.astype(_F32), lo.astype(_F32)],
                              axis=0).astype(_BF16)
        for h in range(H_B):
            hs = slice(h * DK_B, (h + 1) * DK_B)
            s0 = sh_ref[0, u, h]
            o_s[pl.ds(r, LS), hs] += _dot(qe[:, hs], s0.astype(_BF16))
            rhs = jnp.concatenate([jnp.concatenate([vv[:, hs], zpad], axis=1), rhs_sel], axis=0).astype(_BF16)
            both = _dot_tn(lhs[:, hs], rhs)
            hgrn_ref[0, u, h] = jnp.exp(both[:, DV_B:]) * s0 + both[:, :DV_B]

    @pl.when(j == nj - 1)
    def _():
        c0 = 3 * D_A + 2 * HK_B + D_B
        b_gate = _dot(xn_s[...], win_ref[:, c0:c0 + D_B])
        for h in range(H_B):
            hs = slice(h * DV_B, (h + 1) * DV_B)
            cat_s[:, D_A + h * DV_B:D_A + (h + 1) * DV_B] = (
                _rms(o_s[:, hs], og_ref[:, hs]) * _silu(b_gate[:, hs])).astype(_BF16)
        y_ref[...] = x_ref[...] + _dot(cat_s[...], wout_ref[...])


def _layer0_sample(x, sconv, shgrn, gn, win, cw, cb, lng, lnb, lbl, og, wout):
    NB, LS, _ = x.shape
    T = ROW_TILE
    G = T // LS
    nsteps = G // SEQ_PER_STEP
    d_in = win.shape[1]
    xr = x.reshape(NB * LS, D_MODEL)
    kern = functools.partial(_l0_sample_kernel, T=T, LS=LS)
    y, conv_new, hgrn_new = pl.pallas_call(
        kern,
        grid=(NB // G, nsteps),
        in_specs=[
            pl.BlockSpec((T, D_MODEL), lambda i, j: (i, 0)),
            pl.BlockSpec((1, G, CONV_HIST, D_A), lambda i, j: (0, i, 0, 0)),
            pl.BlockSpec((1, SEQ_PER_STEP, H_B, DK_B, DV_B), lambda i, j: (0, i * nsteps + j, 0, 0, 0)),
            _const_spec((1, D_MODEL)),
            _const_spec((D_MODEL, d_in)),
            _const_spec((HIST_ROWS, D_A)),
            _const_spec((1, D_A)),
            _const_spec((1, D_A)),
            _const_spec((1, D_A)),
            _const_spec(lbl.shape),
            _const_spec((1, D_B)),
            _const_spec((D_A + D_B, D_MODEL)),
        ],
        out_specs=[
            pl.BlockSpec((T, D_MODEL), lambda i, j: (i, 0)),
            pl.BlockSpec((1, G, CONV_HIST, D_A), lambda i, j: (0, i, 0, 0)),
            pl.BlockSpec((1, SEQ_PER_STEP, H_B, DK_B, DV_B), lambda i, j: (0, i * nsteps + j, 0, 0, 0)),
        ],
        out_shape=[
            jax.ShapeDtypeStruct((NB * LS, D_MODEL), _F32),
            jax.ShapeDtypeStruct(sconv.shape, _F32),
            jax.ShapeDtypeStruct(shgrn.shape, _F32),
        ],
        scratch_shapes=[
            pltpu.VMEM((T, D_MODEL), _BF16),
            pltpu.VMEM((T, D_A), _F32),
            pltpu.VMEM((XP_ROWS, D_A), _F32),
            pltpu.VMEM((T, D_A), _F32),
            pltpu.VMEM((T, HK_B), _F32),
            pltpu.VMEM((T, HK_B), _F32),
            pltpu.VMEM((T, D_B), _F32),
            pltpu.VMEM((T, HK_B), _F32),
            pltpu.VMEM((T, D_B), _F32),
            pltpu.VMEM((T, D_A + D_B), _BF16),
        ],
        compiler_params=pltpu.CompilerParams(
            dimension_semantics=("arbitrary", "arbitrary"),
            vmem_limit_bytes=VMEM_LIMIT_BYTES),
        name="layer0_sample",
    )(xr, sconv, shgrn, gn, win, cw, cb, lng, lnb, lbl, og, wout)
    return y.reshape(NB, LS, D_MODEL), conv_new, hgrn_new


def _l1_kernel(x_ref, gn_ref, win_ref, lng_ref, lnb_ref, wmix_ref, bmix_ref, wout_ref, fn_ref,
               y_ref, v_ref, xn_s, z_s, vb_s, ug_s, p_s, *rest, T, R, tiles_per_seq):
    i = pl.program_id(0)
    RBLK = min(T, 256)
    nblk = T // RBLK
    zero_of = lambda t: jnp.minimum(jnp.abs(t), 0.0)

    ri = lax.broadcasted_iota(jnp.int32, (CHUNK_C, CHUNK_C), 0)
    ci = lax.broadcasted_iota(jnp.int32, (CHUNK_C, CHUNK_C), 1)
    mask = ((ri ^ ci) < R) & (ci <= ri)
    wms = []
    for h in range(H_C):
        wg = wmix_ref[h]
        if R < CHUNK_C:
            lane = lax.broadcasted_iota(jnp.int32, wg.shape, 1)
            wg = jnp.where(lane < R, wg, 0.0)
            span = R
            while span < CHUNK_C:
                wg = wg + pltpu.roll(wg, span, 1)
                span *= 2
            wg = jnp.concatenate([wg] * (CHUNK_C // R), axis=0)
        wms.append(jnp.where(mask, wg, 0.0).astype(_BF16))

    slot = lambda r: slice((r % 2) * RBLK, (r % 2 + 1) * RBLK)
    CG = DG_C
    n_dot = 3 * D_C // CG

    def norm_rows(r):
        rows = slice(r * RBLK, (r + 1) * RBLK)
        xn_s[slot(r)] = _rms(x_ref[rows], gn_ref[...]).astype(_BF16)

    def proj_group(r, k):
        z = _dot(xn_s[slot(r)], win_ref[:, k * CG:(k + 1) * CG])
        z_s[slot(r), k * CG:(k + 1) * CG] = z
        return zero_of(z[0:SUBLANES, 0:LANES])

    def with_dep(val, dep):
        top = jnp.concatenate([val[0:SUBLANES, 0:LANES] + dep, val[0:SUBLANES, LANES:]], axis=1)
        return jnp.concatenate([top, val[SUBLANES:]], axis=0)

    def back(r, dep, nxt):
        rows = slice(r * RBLK, (r + 1) * RBLK)
        sl = slot(r)
        if nxt is not None:
            norm_rows(nxt)
        n_piece = 2 * H_C
        emitted = 0
        dot_deps = {}
        s1 = jnp.zeros((RBLK, 1), _F32)
        for pi in range(n_piece):
            if pi - 2 in dot_deps:
                dep = dep + dot_deps.pop(pi - 2)
            g = pi // 2
            cs = slice(g * CG, (g + 1) * CG)
            if pi % 2 == 0:
                zu = with_dep(z_s[sl, g * CG:(g + 1) * CG], dep)
                ug = _gelu(zu) * _silu(z_s[sl, 2 * D_C + g * CG:2 * D_C + (g + 1) * CG])
                ug_s[sl, cs] = ug
                dep = zero_of(ug[0:SUBLANES, 0:LANES])
            else:
                gv = _gelu(with_dep(z_s[sl, D_C + g * CG:D_C + (g + 1) * CG], dep))
                z_s[sl, D_C + g * CG:D_C + (g + 1) * CG] = gv
                s1 = s1 + jnp.sum(gv, axis=-1, keepdims=True)
                dep = zero_of(gv[0:SUBLANES, 0:LANES])
            if nxt is not None:
                while emitted < ((pi + 1) * n_dot) // n_piece:
                    d = proj_group(nxt, emitted)
                    dot_deps[pi] = d if pi not in dot_deps else dot_deps[pi] + d
                    emitted += 1
        for d in dot_deps.values():
            dep = dep + d
        c = z_s[sl, D_C:2 * D_C] - s1 * (1.0 / D_C)
        v = c * lax.rsqrt(jnp.mean(c * c, axis=-1, keepdims=True) + EPS) * lng_ref[...] + lnb_ref[...]
        vb_s[sl] = v.astype(_BF16)
        if tiles_per_seq is None:
            v_ref[rows] = v
        elif r == nblk - 1:
            rest[0][...] = v[RBLK - CHUNK_C:RBLK]
        for h in range(H_C):
            hs = slice(h * DG_C, (h + 1) * DG_C)
            for gr in range(RBLK // CHUNK_C):
                rs = slice(sl.start + gr * CHUNK_C, sl.start + (gr + 1) * CHUNK_C)
                mix = _dot(wms[h], vb_s[rs, hs]) + bmix_ref[h]
                p_s[rs, hs] = (ug_s[rs, hs] * mix).astype(_BF16)
        y = x_ref[rows] + _dot(p_s[sl], wout_ref[...])
        y_ref[rows] = _rms(y, fn_ref[...])
        return dep + zero_of(y[0:SUBLANES, 0:LANES])

    norm_rows(0)
    for k in range(n_dot):
        proj_group(0, k)
    dep = jnp.zeros((SUBLANES, LANES), _F32)
    for r in range(nblk):
        dep = back(r, dep, r + 1 if r + 1 < nblk else None)

    if tiles_per_seq is not None:
        @pl.when(i % tiles_per_seq == tiles_per_seq - 1)
        def _():
            v_ref[0, 0] = rest[0][...]


def _layer1(xr, gn, win, lng, lnb, wmix, bmix, wout, fn, *, R, seq_len):
    N = xr.shape[0]
    T = L1_SAMPLE_ROW_TILE if seq_len is None else L1_PROMPT_ROW_TILE
    srows = min(T, 2 * 256)
    if seq_len is None:
        tiles_per_seq = None
        v_shape = (N, D_C)
        v_spec = pl.BlockSpec((T, D_C), lambda i: (i, 0))
    else:
        tiles_per_seq = seq_len // T
        v_shape = (1, N // seq_len, CHUNK_C, D_C)
        v_spec = pl.BlockSpec((1, 1, CHUNK_C, D_C), lambda i: (0, i // tiles_per_seq, 0, 0))
    kern = functools.partial(_l1_kernel, T=T, R=R, tiles_per_seq=tiles_per_seq)
    return pl.pallas_call(
        kern,
        grid=(N // T,),
        in_specs=[
            pl.BlockSpec((T, D_MODEL), lambda i: (i, 0)),
            _const_spec((1, D_MODEL)),
            _const_spec((D_MODEL, 3 * D_C)),
            _const_spec((1, D_C)),
            _const_spec((1, D_C)),
            _const_spec(wmix.shape),
            _const_spec((H_C, CHUNK_C, 1)),
            _const_spec((D_C, D_MODEL)),
            _const_spec((1, D_MODEL)),
        ],
        out_specs=[pl.BlockSpec((T, D_MODEL), lambda i: (i, 0)), v_spec],
        out_shape=[jax.ShapeDtypeStruct((N, D_MODEL), _F32), jax.ShapeDtypeStruct(v_shape, _F32)],
        scratch_shapes=[
            pltpu.VMEM((srows, D_MODEL), _BF16),
            pltpu.VMEM((srows, 3 * D_C), _F32),
            pltpu.VMEM((srows, D_C), _BF16),
            pltpu.VMEM((srows, D_C), _F32),
            pltpu.VMEM((srows, D_C), _BF16),
        ] + ([] if seq_len is None else [pltpu.VMEM((CHUNK_C, D_C), _F32)]),
        compiler_params=pltpu.CompilerParams(
            dimension_semantics=("arbitrary",),
            vmem_limit_bytes=VMEM_LIMIT_BYTES),
        name="layer1_prompt" if seq_len is not None else "layer1_sample",
    )(xr, gn, win, lng, lnb, wmix, bmix, wout, fn)


def kernel(x_prompt, x_sample, state_conv, state_hgrn, norm_ab, w_in_ab, conv_w, conv_b, ln_a_g, ln_a_b, lb_logits, onorm_b, w_out_ab, norm_c, w_in_c, ln_c_g, ln_c_b, w_s, b_s, w_out_c, final_norm):
    B, L, _ = x_prompt.shape
    NB, LS, _ = x_sample.shape
    row = lambda p: p.reshape(1, -1)

    gn0 = row(norm_ab[0])
    win0 = w_in_ab[0].astype(_BF16)
    wout0 = w_out_ab[0].astype(_BF16)
    cw = jnp.pad(conv_w[0], ((0, HIST_ROWS - CONV_W), (0, 0)))
    cb, lng, lnb, og = row(conv_b[0]), row(ln_a_g[0]), row(ln_a_b[0]), row(onorm_b[0])
    l0_params = (gn0, win0, cw, cb, lng, lnb, lb_logits, og, wout0)

    xp1, conv_prompt, hgrn_prompt = _layer0_prompt(x_prompt, *l0_params)
    xs1, conv_sample, hgrn_sample = _layer0_sample(x_sample, state_conv, state_hgrn, *l0_params)

    gn1 = row(norm_c[0])
    win1 = w_in_c[0].astype(_BF16)
    wout1 = w_out_c[0].astype(_BF16)
    lcg, lcb, fn = row(ln_c_g[0]), row(ln_c_b[0]), row(final_norm)
    reps = CHUNK_C // LS
    wmix_p, bmix_p = w_s[0], b_s[0][:, :, None]
    wmix_s = w_s[0][:, :LS, :]
    bmix_s = jnp.tile(b_s[0][:, :LS], (1, reps))[:, :, None]

    y_prompt, v_prompt = _layer1(xp1.reshape(B * L, D_MODEL), gn1, win1, lcg, lcb, wmix_p, bmix_p, wout1, fn,
                                 R=CHUNK_C, seq_len=L)
    y_sample, v_sample = _layer1(xs1.reshape(NB * LS, D_MODEL), gn1, win1, lcg, lcb, wmix_s, bmix_s, wout1, fn,
                                 R=LS, seq_len=None)

    return (y_prompt.reshape(B, L, D_MODEL), y_sample.reshape(NB, LS, D_MODEL),
            conv_prompt, hgrn_prompt, v_prompt,
            conv_sample, hgrn_sample, v_sample.reshape(1, NB, LS, D_C))
```

```python
import functools

import jax
import jax.numpy as jnp
from jax import lax
from jax.experimental import pallas as pl
from jax.experimental.pallas import tpu as pltpu

D_MODEL = 1024
D_A = D_MODEL
CONV_W = 31
CONV_HIST = CONV_W - 1
H_B = 8
DK_B = 128
DV_B = 128
D_B = H_B * DV_B
HK_B = H_B * DK_B
CHUNK_B = 64
D_C = 2 * D_MODEL
H_C = 8
DG_C = D_C // H_C
CHUNK_C = 128
EPS = 1e-6

SUBLANES = 8
LANES = 128
HIST_ROWS = 32
ROW_TILE = 256
PROMPT_ROW_TILE = 512
L1_PROMPT_ROW_TILE = 512
L1_SAMPLE_ROW_TILE = 512
VMEM_LIMIT_BYTES = 60 * 1024 * 1024

_F32 = jnp.float32
_BF16 = jnp.bfloat16

_NT = (((1,), (1,)), ((), ()))
_TN = (((0,), (0,)), ((), ()))


def _dot(a, b):
    return jnp.dot(a, b, preferred_element_type=_F32)


def _dot_nt(a, b):
    return lax.dot_general(a, b, _NT, preferred_element_type=_F32)


def _dot_tn(a, b):
    return lax.dot_general(a, b, _TN, preferred_element_type=_F32)


def _rms(x, g):
    return x * lax.rsqrt(jnp.mean(x * x, axis=-1, keepdims=True) + EPS) * g


def _layernorm(x, g, b):
    xc = x - jnp.mean(x, axis=-1, keepdims=True)
    return xc * lax.rsqrt(jnp.mean(xc * xc, axis=-1, keepdims=True) + EPS) * g + b


def _silu(x):
    hx = 0.5 * x
    return hx * jnp.tanh(hx) + hx


_GELU_C = 0.7978845608028654
_GELU_C3 = _GELU_C * 0.044715


def _gelu(x):
    hx = 0.5 * x
    return hx * jnp.tanh(x * (_GELU_C3 * (x * x) + _GELU_C)) + hx


def _split3(x):
    hi = x.astype(_BF16)
    r1 = x - hi.astype(_F32)
    mid = r1.astype(_BF16)
    lo = (r1 - mid.astype(_F32)).astype(_BF16)
    return hi, mid, lo


def _dot_exact_lhs(m, x):
    hi, mid, lo = _split3(x)
    return _dot(m, hi) + _dot(m, mid) + _dot(m, lo)


def _forget_lower_bound(lbl_ref):
    l = lbl_ref[...]
    e = jnp.exp(l - jnp.max(l, axis=0, keepdims=True))
    return e[0:1] / jnp.sum(e, axis=0, keepdims=True)


def _l0_prompt_kernel(x_ref, gn_ref, win_ref, cw_ref, cb_ref, lng_ref, lnb_ref, lbl_ref, og_ref, wout_ref,
                      y_ref, conv_ref, hgrn_ref,
                      abuf, st_s, xn_s, yc_s, z_s, lf_s, k_s, v_s, cat_s, *, T):
    t = pl.program_id(1)
    nt = pl.num_programs(1)
    CG = 256
    RBLK = 256
    ANCHOR_SKEW = 4

    @pl.when(t == 0)
    def _():
        abuf[0:HIST_ROWS] = jnp.zeros((HIST_ROWS, D_A), _F32)
        st_s[...] = jnp.zeros_like(st_s)

    off0 = HIST_ROWS - CONV_HIST
    RB = 64
    WIN = RB + HIST_ROWS
    n_lb = D_A // LANES
    z0 = 2 * D_A
    n_dot = (win_ref.shape[1] - z0) // CG
    ZG, ZQ, ZF, ZI, ZB = 0, D_A, D_A + HK_B, D_A + 2 * HK_B, D_A + 2 * HK_B + D_B

    zero_of = lambda v: jnp.minimum(jnp.abs(v), 0.0)

    def conv_block(rb, lb, dep):
        ls = slice(lb * LANES, (lb + 1) * LANES)
        win = abuf[rb * RB:rb * RB + WIN, ls]
        acc = jnp.concatenate([dep, jnp.zeros((RB - SUBLANES, LANES), _F32)], axis=0)
        for s in range(SUBLANES):
            rolled = win if s == 0 else pltpu.roll(win, WIN - s, 0)
            for k in range(CONV_W):
                if (off0 + k) % SUBLANES != s:
                    continue
                base = off0 + k - s
                acc = acc + cw_ref[k:k + 1, ls] * rolled[base:base + RB]
        yc_s[rb * RB:(rb + 1) * RB, ls] = acc
        return zero_of(acc[0:SUBLANES])

    def front(r, dep):
        rs = slice(r * RBLK, (r + 1) * RBLK)
        xn_s[rs] = _rms(x_ref[0, rs], gn_ref[...]).astype(_BF16)
        zdot = lambda c0: _dot(xn_s[rs], win_ref[:, c0:c0 + CG])
        for g in range(D_A // CG):
            a_val = zdot(g * CG)
            a_glu = zdot(D_A + g * CG)
            abuf[HIST_ROWS + r * RBLK:HIST_ROWS + (r + 1) * RBLK, g * CG:(g + 1) * CG] = (
                a_val * jax.nn.sigmoid(a_glu))
        n_conv = (RBLK // RB) * n_lb
        emitted = 0
        dot_deps = {}
        for i in range(n_conv):
            if i - ANCHOR_SKEW in dot_deps:
                dep = dep + dot_deps.pop(i - ANCHOR_SKEW)
            dep = conv_block(r * (RBLK // RB) + i // n_lb, i % n_lb, dep)
            while emitted < ((i + 1) * n_dot) // n_conv:
                z = zdot(z0 + emitted * CG)
                z_s[rs, emitted * CG:(emitted + 1) * CG] = z
                dot_deps[i] = zero_of(z[0:SUBLANES, 0:LANES])
                emitted += 1
        return dep

    C = CHUNK_B
    ri = lax.broadcasted_iota(jnp.int32, (C, C), 0)
    ci = lax.broadcasted_iota(jnp.int32, (C, C), 1)
    tril = (ci <= ri).astype(_BF16)
    rg = lax.broadcasted_iota(jnp.int32, (RBLK, RBLK), 0)
    cg = lax.broadcasted_iota(jnp.int32, (RBLK, RBLK), 1)
    intra = ((rg ^ cg) < C) & (cg <= rg)
    heads = [slice(h * DK_B, (h + 1) * DK_B) for h in range(H_B)]
    lb = _forget_lower_bound(lbl_ref)

    def back(r):
        gs = slice(r * RBLK, (r + 1) * RBLK)
        yc = _layernorm(yc_s[gs] + cb_ref[...], lng_ref[...], lnb_ref[...])
        cat_a = (_silu(yc) * _silu(z_s[gs, ZG:ZG + D_A])).astype(_BF16)
        y_ref[0, gs] = x_ref[0, gs] + _dot(cat_a, wout_ref[0:D_A, :])

        f = lb + (1.0 - lb) * jax.nn.sigmoid(z_s[gs, ZF:ZF + HK_B])
        lf_s[gs] = jnp.log(f)
        k_s[gs] = 1.0 - f
        v_s[gs] = _silu(z_s[gs, ZI:ZI + D_B]).astype(_BF16)

        qa_l, kb_l, qe_l, kc_l, decay_l = [], [], [], [], []
        for c in range(RBLK // C):
            rs = slice(r * RBLK + c * C, r * RBLK + (c + 1) * C)
            bcum = _dot_exact_lhs(tril, lf_s[rs])
            bmid = bcum[C // 2:C // 2 + 1]
            bend = bcum[C - 1:C]
            d = bcum - bmid
            qa = z_s[rs, ZQ:ZQ + HK_B] * jnp.exp(d)
            kb = k_s[rs] * jnp.exp(-d)
            qe_l.append((qa * jnp.exp(bmid)).astype(_BF16))
            kc_l.append((kb * jnp.exp(bend - bmid)).astype(_BF16))
            qa_l.append(qa.astype(_BF16))
            kb_l.append(kb.astype(_BF16))
            decay_l.append(jnp.exp(bend))
        qa = jnp.concatenate(qa_l, axis=0)
        kb = jnp.concatenate(kb_l, axis=0)
        o_intra = []
        for hs in heads:
            sc = jnp.where(intra, _dot_nt(qa[:, hs], kb[:, hs]), 0.0).astype(_BF16)
            o_intra.append(_dot(sc, v_s[gs, hs]))
        for c in range(RBLK // C):
            rs = slice(r * RBLK + c * C, r * RBLK + (c + 1) * C)
            for h, hs in enumerate(heads):
                st = st_s[h]
                o = o_intra[h][c * C:(c + 1) * C] + _dot_nt(qe_l[c][:, hs], st.astype(_BF16))
                st_s[h] = st * decay_l[c][:, hs] + _dot_tn(v_s[rs, hs], kc_l[c][:, hs])
                gate = _silu(z_s[rs, ZB + h * DV_B:ZB + (h + 1) * DV_B])
                cat_s[rs, hs] = (_rms(o, og_ref[:, hs]) * gate).astype(_BF16)
        y_ref[0, gs] += _dot(cat_s[gs], wout_ref[D_A:D_A + D_B, :])

    dep = jnp.zeros((SUBLANES, LANES), _F32)
    dep = front(0, dep)
    for r in range(T // RBLK):
        if r + 1 < T // RBLK:
            dep = front(r + 1, dep)
        back(r)

    abuf[0:HIST_ROWS] = abuf[T:T + HIST_ROWS]

    @pl.when(t == nt - 1)
    def _():
        conv_ref[0, 0] = abuf[T + off0:T + HIST_ROWS]
        for h in range(H_B):
            hgrn_ref[0, 0, h] = st_s[h].T


def _const_spec(shape):
    nd = len(shape)
    return pl.BlockSpec(shape, lambda *_: (0,) * nd, pipeline_mode=pl.Buffered(1))


def _layer0_prompt(x, gn, win, cw, cb, lng, lnb, lbl, og, wout):
    B, L, _ = x.shape
    T = PROMPT_ROW_TILE
    d_in = win.shape[1]
    kern = functools.partial(_l0_prompt_kernel, T=T)
    return pl.pallas_call(
        kern,
        grid=(B, L // T),
        in_specs=[
            pl.BlockSpec((1, T, D_MODEL), lambda b, t: (b, t, 0)),
            _const_spec((1, D_MODEL)),
            _const_spec((D_MODEL, d_in)),
            _const_spec((HIST_ROWS, D_A)),
            _const_spec((1, D_A)),
            _const_spec((1, D_A)),
            _const_spec((1, D_A)),
            _const_spec(lbl.shape),
            _const_spec((1, D_B)),
            _const_spec((D_A + D_B, D_MODEL)),
        ],
        out_specs=[
            pl.BlockSpec((1, T, D_MODEL), lambda b, t: (b, t, 0)),
            pl.BlockSpec((1, 1, CONV_HIST, D_A), lambda b, t: (0, b, 0, 0)),
            pl.BlockSpec((1, 1, H_B, DK_B, DV_B), lambda b, t: (0, b, 0, 0, 0)),
        ],
        out_shape=[
            jax.ShapeDtypeStruct((B, L, D_MODEL), _F32),
            jax.ShapeDtypeStruct((1, B, CONV_HIST, D_A), _F32),
            jax.ShapeDtypeStruct((1, B, H_B, DK_B, DV_B), _F32),
        ],
        scratch_shapes=[
            pltpu.VMEM((T + HIST_ROWS, D_A), _F32),
            pltpu.VMEM((H_B, DV_B, DK_B), _F32),
            pltpu.VMEM((T, D_MODEL), _BF16),
            pltpu.VMEM((T, D_A), _F32),
            pltpu.VMEM((T, d_in - 2 * D_A), _F32),
            pltpu.VMEM((T, HK_B), _F32),
            pltpu.VMEM((T, HK_B), _F32),
            pltpu.VMEM((T, D_B), _BF16),
            pltpu.VMEM((T, D_B), _BF16),
        ],
        compiler_params=pltpu.CompilerParams(
            dimension_semantics=("arbitrary", "arbitrary"),
            vmem_limit_bytes=VMEM_LIMIT_BYTES),
        name="layer0_prompt",
    )(x, gn, win, cw, cb, lng, lnb, lbl, og, wout)


SEQ_PER_STEP = 4
XP_ROWS = 40


def _l0_sample_kernel(x_ref, sc_ref, sh_ref, gn_ref, win_ref, cw_ref, cb_ref, lng_ref, lnb_ref, lbl_ref,
                      og_ref, wout_ref,
                      y_ref, conv_ref, hgrn_ref,
                      xn_s, a_s, xp_s, yc_s, qe_s, kc_s, v_s, lf_s, o_s, cat_s, *, T, LS):
    j = pl.program_id(1)
    nj = pl.num_programs(1)
    G = T // LS

    @pl.when(j == 0)
    def _():
        x = x_ref[...]
        xn_s[...] = _rms(x, gn_ref[...]).astype(_BF16)
        a_val = _dot(xn_s[...], win_ref[:, 0:D_A])
        a_glu = _dot(xn_s[...], win_ref[:, D_A:2 * D_A])
        a_s[...] = a_val * jax.nn.sigmoid(a_glu)

        def conv_seq(g, carry):
            r = pl.multiple_of(g * LS, LS)
            xp_s[0:CONV_HIST] = sc_ref[0, g]
            xp_s[CONV_HIST:CONV_HIST + LS] = a_s[pl.ds(r, LS)]
            conv_ref[0, g] = xp_s[LS:LS + CONV_HIST]
            for lb in range(D_A // LANES):
                ls = slice(lb * LANES, (lb + 1) * LANES)
                win = xp_s[:, ls]
                acc = jnp.zeros((LS, LANES), _F32)
                for s in range(SUBLANES):
                    rolled = win if s == 0 else pltpu.roll(win, XP_ROWS - s, 0)
                    for k in range(CONV_W):
                        if k % SUBLANES != s:
                            continue
                        acc = acc + cw_ref[k:k + 1, ls] * rolled[k - s:k - s + LS]
                yc_s[pl.ds(r, LS), ls] = acc
            return carry

        xp_s[CONV_HIST + LS:XP_ROWS] = jnp.zeros((XP_ROWS - CONV_HIST - LS, D_A), _F32)
        lax.fori_loop(0, G, conv_seq, 0)

        yc = _layernorm(yc_s[...] + cb_ref[...], lng_ref[...], lnb_ref[...])
        a_gate = _dot(xn_s[...], win_ref[:, 2 * D_A:3 * D_A])
        cat_s[:, 0:D_A] = (_silu(yc) * _silu(a_gate)).astype(_BF16)

        c0 = 3 * D_A
        q = _dot(xn_s[...], win_ref[:, c0:c0 + HK_B])
        lb = _forget_lower_bound(lbl_ref)
        f = lb + (1.0 - lb) * jax.nn.sigmoid(_dot(xn_s[...], win_ref[:, c0 + HK_B:c0 + 2 * HK_B]))
        lf_s[...] = jnp.log(f)
        kk = 1.0 - f
        v_s[...] = _silu(_dot(xn_s[...], win_ref[:, c0 + 2 * HK_B:c0 + 2 * HK_B + D_B]))

        ri = lax.broadcasted_iota(jnp.int32, (LANES, LANES), 0)
        ci = lax.broadcasted_iota(jnp.int32, (LANES, LANES), 1)
        same = (ri ^ ci) < LS
        causal = same & (ci <= ri)
        m_cum = causal.astype(_BF16)
        m_mid = (same & ((ci & (LS - 1)) <= LS // 2)).astype(_BF16)
        m_end = same.astype(_BF16)
        for gr in range(T // LANES):
            rs = slice(gr * LANES, (gr + 1) * LANES)
            hi, mid, lo = _split3(lf_s[rs])
            ex = lambda m: _dot(m, hi) + _dot(m, mid) + _dot(m, lo)
            bcum, bmid, bend = ex(m_cum), ex(m_mid), ex(m_end)
            d = bcum - bmid
            qa = q[rs] * jnp.exp(d)
            kb = kk[rs] * jnp.exp(-d)
            qe_s[rs] = qa * jnp.exp(bmid)
            kc_s[rs] = kb * jnp.exp(bend - bmid)
            qa = qa.astype(_BF16)
            kb = kb.astype(_BF16)
            for h in range(H_B):
                hs = slice(h * DK_B, (h + 1) * DK_B)
                sc = jnp.where(causal, _dot_nt(qa[:, hs], kb[:, hs]), 0.0)
                o_s[rs, hs] = _dot(sc.astype(_BF16), v_s[rs, hs].astype(_BF16))

    rhs_sel = jnp.concatenate([jnp.zeros((3 * LS, DV_B), _F32), jnp.ones((3 * LS, DV_B), _F32)], axis=1)
    zpad = jnp.zeros((LS, DV_B), _F32)
    for u in range(SEQ_PER_STEP):
        r = pl.multiple_of((j * SEQ_PER_STEP + u) * LS, LS)
        qe = qe_s[pl.ds(r, LS)].astype(_BF16)
        vv = v_s[pl.ds(r, LS)]
        hi, mid, lo = _split3(lf_s[pl.ds(r, LS)])
        lhs = jnp.concatenate([kc_s[pl.ds(r, LS)], hi.astype(_F32), mid.astype(_F32), lo.astype(_F32)],
                              axis=0).astype(_BF16)
        for h in range(H_B):
            hs = slice(h * DK_B, (h + 1) * DK_B)
            s0 = sh_ref[0, u, h]
            o_s[pl.ds(r, LS), hs] += _dot(qe[:, hs], s0.astype(_BF16))
            rhs = jnp.concatenate([jnp.concatenate([vv[:, hs], zpad], axis=1), rhs_sel], axis=0).astype(_BF16)
            both = _dot_tn(lhs[:, hs], rhs)
            hgrn_ref[0, u, h] = jnp.exp(both[:, DV_B:]) * s0 + both[:, :DV_B]

    @pl.when(j == nj - 1)
    def _():
        c0 = 3 * D_A + 2 * HK_B + D_B
        b_gate = _dot(xn_s[...], win_ref[:, c0:c0 + D_B])
        for h in range(H_B):
            hs = slice(h * DV_B, (h + 1) * DV_B)
            cat_s[:, D_A + h * DV_B:D_A + (h + 1) * DV_B] = (
                _rms(o_s[:, hs], og_ref[:, hs]) * _silu(b_gate[:, hs])).astype(_BF16)
        y_ref[...] = x_ref[...] + _dot(cat_s[...], wout_ref[...])


def _layer0_sample(x, sconv, shgrn, gn, win, cw, cb, lng, lnb, lbl, og, wout):
    NB, LS, _ = x.shape
    T = ROW_TILE
    G = T // LS
    nsteps = G // SEQ_PER_STEP
    d_in = win.shape[1]
    xr = x.reshape(NB * LS, D_MODEL)
    kern = functools.partial(_l0_sample_kernel, T=T, LS=LS)
    y, conv_new, hgrn_new = pl.pallas_call(
        kern,
        grid=(NB // G, nsteps),
        in_specs=[
            pl.BlockSpec((T, D_MODEL), lambda i, j: (i, 0)),
            pl.BlockSpec((1, G, CONV_HIST, D_A), lambda i, j: (0, i, 0, 0)),
            pl.BlockSpec((1, SEQ_PER_STEP, H_B, DK_B, DV_B), lambda i, j: (0, i * nsteps + j, 0, 0, 0)),
            _const_spec((1, D_MODEL)),
            _const_spec((D_MODEL, d_in)),
            _const_spec((HIST_ROWS, D_A)),
            _const_spec((1, D_A)),
            _const_spec((1, D_A)),
            _const_spec((1, D_A)),
            _const_spec(lbl.shape),
            _const_spec((1, D_B)),
            _const_spec((D_A + D_B, D_MODEL)),
        ],
        out_specs=[
            pl.BlockSpec((T, D_MODEL), lambda i, j: (i, 0)),
            pl.BlockSpec((1, G, CONV_HIST, D_A), lambda i, j: (0, i, 0, 0)),
            pl.BlockSpec((1, SEQ_PER_STEP, H_B, DK_B, DV_B), lambda i, j: (0, i * nsteps + j, 0, 0, 0)),
        ],
        out_shape=[
            jax.ShapeDtypeStruct((NB * LS, D_MODEL), _F32),
            jax.ShapeDtypeStruct(sconv.shape, _F32),
            jax.ShapeDtypeStruct(shgrn.shape, _F32),
        ],
        scratch_shapes=[
            pltpu.VMEM((T, D_MODEL), _BF16),
            pltpu.VMEM((T, D_A), _F32),
            pltpu.VMEM((XP_ROWS, D_A), _F32),
            pltpu.VMEM((T, D_A), _F32),
            pltpu.VMEM((T, HK_B), _F32),
            pltpu.VMEM((T, HK_B), _F32),
            pltpu.VMEM((T, D_B), _F32),
            pltpu.VMEM((T, HK_B), _F32),
            pltpu.VMEM((T, D_B), _F32),
            pltpu.VMEM((T, D_A + D_B), _BF16),
        ],
        compiler_params=pltpu.CompilerParams(
            dimension_semantics=("arbitrary", "arbitrary"),
            vmem_limit_bytes=VMEM_LIMIT_BYTES),
        name="layer0_sample",
    )(xr, sconv, shgrn, gn, win, cw, cb, lng, lnb, lbl, og, wout)
    return y.reshape(NB, LS, D_MODEL), conv_new, hgrn_new


def _l1_kernel(x_ref, gn_ref, win_ref, lng_ref, lnb_ref, wmix_ref, bmix_ref, wout_ref, fn_ref,
               y_ref, v_ref, xn_s, z_s, vb_s, ug_s, p_s, *rest, T, R, tiles_per_seq):
    i = pl.program_id(0)
    RBLK = min(T, 256)
    nblk = T // RBLK
    zero_of = lambda t: jnp.minimum(jnp.abs(t), 0.0)

    ri = lax.broadcasted_iota(jnp.int32, (CHUNK_C, CHUNK_C), 0)
    ci = lax.broadcasted_iota(jnp.int32, (CHUNK_C, CHUNK_C), 1)
    mask = ((ri ^ ci) < R) & (ci <= ri)
    wms = []
    for h in range(H_C):
        wg = wmix_ref[h]
        if R < CHUNK_C:
            lane = lax.broadcasted_iota(jnp.int32, wg.shape, 1)
            wg = jnp.where(lane < R, wg, 0.0)
            span = R
            while span < CHUNK_C:
                wg = wg + pltpu.roll(wg, span, 1)
                span *= 2
            wg = jnp.concatenate([wg] * (CHUNK_C // R), axis=0)
        wms.append(jnp.where(mask, wg, 0.0).astype(_BF16))

    slot = lambda r: slice((r % 2) * RBLK, (r % 2 + 1) * RBLK)
    CG = DG_C
    n_dot = 3 * D_C // CG
    PIECE_SKEW = 3

    def norm_rows(r):
        rows = slice(r * RBLK, (r + 1) * RBLK)
        xn_s[slot(r)] = _rms(x_ref[rows], gn_ref[...]).astype(_BF16)

    def proj_group(r, k):
        z = _dot(xn_s[slot(r)], win_ref[:, k * CG:(k + 1) * CG])
        z_s[slot(r), k * CG:(k + 1) * CG] = z
        return zero_of(z[0:SUBLANES, 0:LANES])

    def with_dep(val, dep):
        top = jnp.concatenate([val[0:SUBLANES, 0:LANES] + dep, val[0:SUBLANES, LANES:]], axis=1)
        return jnp.concatenate([top, val[SUBLANES:]], axis=0)

    def back(r, dep, nxt):
        rows = slice(r * RBLK, (r + 1) * RBLK)
        sl = slot(r)
        if nxt is not None:
            norm_rows(nxt)
        n_piece = 2 * H_C
        emitted = 0
        dot_deps = {}
        s1 = jnp.zeros((RBLK, 1), _F32)
        for pi in range(n_piece):
            if pi - PIECE_SKEW in dot_deps:
                dep = dep + dot_deps.pop(pi - PIECE_SKEW)
            g = pi // 2
            cs = slice(g * CG, (g + 1) * CG)
            if pi % 2 == 0:
                zu = with_dep(z_s[sl, g * CG:(g + 1) * CG], dep)
                ug = _gelu(zu) * _silu(z_s[sl, 2 * D_C + g * CG:2 * D_C + (g + 1) * CG])
                ug_s[sl, cs] = ug
                dep = zero_of(ug[0:SUBLANES, 0:LANES])
            else:
                gv = _gelu(with_dep(z_s[sl, D_C + g * CG:D_C + (g + 1) * CG], dep))
                z_s[sl, D_C + g * CG:D_C + (g + 1) * CG] = gv
                s1 = s1 + jnp.sum(gv, axis=-1, keepdims=True)
                dep = zero_of(gv[0:SUBLANES, 0:LANES])
            if nxt is not None:
                while emitted < ((pi + 1) * n_dot) // n_piece:
                    d = proj_group(nxt, emitted)
                    dot_deps[pi] = d if pi not in dot_deps else dot_deps[pi] + d
                    emitted += 1
        for d in dot_deps.values():
            dep = dep + d
        c = z_s[sl, D_C:2 * D_C] - s1 * (1.0 / D_C)
        v = c * lax.rsqrt(jnp.mean(c * c, axis=-1, keepdims=True) + EPS) * lng_ref[...] + lnb_ref[...]
        vb_s[sl] = v.astype(_BF16)
        if tiles_per_seq is None:
            v_ref[rows] = v
        elif r == nblk - 1:
            rest[0][...] = v[RBLK - CHUNK_C:RBLK]
        for h in range(H_C):
            hs = slice(h * DG_C, (h + 1) * DG_C)
            for gr in range(RBLK // CHUNK_C):
                rs = slice(sl.start + gr * CHUNK_C, sl.start + (gr + 1) * CHUNK_C)
                mix = _dot(wms[h], vb_s[rs, hs]) + bmix_ref[h]
                p_s[rs, hs] = (ug_s[rs, hs] * mix).astype(_BF16)
        y = x_ref[rows] + _dot(p_s[sl], wout_ref[...])
        y_ref[rows] = _rms(y, fn_ref[...])
        return dep + zero_of(y[0:SUBLANES, 0:LANES])

    norm_rows(0)
    for k in range(n_dot):
        proj_group(0, k)
    dep = jnp.zeros((SUBLANES, LANES), _F32)
    for r in range(nblk):
        dep = back(r, dep, r + 1 if r + 1 < nblk else None)

    if tiles_per_seq is not None:
        @pl.when(i % tiles_per_seq == tiles_per_seq - 1)
        def _():
            v_ref[0, 0] = rest[0][...]


def _layer1(xr, gn, win, lng, lnb, wmix, bmix, wout, fn, *, R, seq_len):
    N = xr.shape[0]
    T = L1_SAMPLE_ROW_TILE if seq_len is None else L1_PROMPT_ROW_TILE
    srows = min(T, 2 * 256)
    if seq_len is None:
        tiles_per_seq = None
        v_shape = (N, D_C)
        v_spec = pl.BlockSpec((T, D_C), lambda i: (i, 0))
    else:
        tiles_per_seq = seq_len // T
        v_shape = (1, N // seq_len, CHUNK_C, D_C)
        v_spec = pl.BlockSpec((1, 1, CHUNK_C, D_C), lambda i: (0, i // tiles_per_seq, 0, 0))
    kern = functools.partial(_l1_kernel, T=T, R=R, tiles_per_seq=tiles_per_seq)
    return pl.pallas_call(
        kern,
        grid=(N // T,),
        in_specs=[
            pl.BlockSpec((T, D_MODEL), lambda i: (i, 0)),
            _const_spec((1, D_MODEL)),
            _const_spec((D_MODEL, 3 * D_C)),
            _const_spec((1, D_C)),
            _const_spec((1, D_C)),
            _const_spec(wmix.shape),
            _const_spec((H_C, CHUNK_C, 1)),
            _const_spec((D_C, D_MODEL)),
            _const_spec((1, D_MODEL)),
        ],
        out_specs=[pl.BlockSpec((T, D_MODEL), lambda i: (i, 0)), v_spec],
        out_shape=[jax.ShapeDtypeStruct((N, D_MODEL), _F32), jax.ShapeDtypeStruct(v_shape, _F32)],
        scratch_shapes=[
            pltpu.VMEM((srows, D_MODEL), _BF16),
            pltpu.VMEM((srows, 3 * D_C), _F32),
            pltpu.VMEM((srows, D_C), _BF16),
            pltpu.VMEM((srows, D_C), _F32),
            pltpu.VMEM((srows, D_C), _BF16),
        ] + ([] if seq_len is None else [pltpu.VMEM((CHUNK_C, D_C), _F32)]),
        compiler_params=pltpu.CompilerParams(
            dimension_semantics=("arbitrary",),
            vmem_limit_bytes=VMEM_LIMIT_BYTES),
        name="layer1_prompt" if seq_len is not None else "layer1_sample",
    )(xr, gn, win, lng, lnb, wmix, bmix, wout, fn)


def kernel(x_prompt, x_sample, state_conv, state_hgrn, norm_ab, w_in_ab, conv_w, conv_b, ln_a_g, ln_a_b, lb_logits, onorm_b, w_out_ab, norm_c, w_in_c, ln_c_g, ln_c_b, w_s, b_s, w_out_c, final_norm):
    B, L, _ = x_prompt.shape
    NB, LS, _ = x_sample.shape
    row = lambda p: p.reshape(1, -1)

    gn0 = row(norm_ab[0])
    win0 = w_in_ab[0].astype(_BF16)
    wout0 = w_out_ab[0].astype(_BF16)
    cw = jnp.pad(conv_w[0], ((0, HIST_ROWS - CONV_W), (0, 0)))
    cb, lng, lnb, og = row(conv_b[0]), row(ln_a_g[0]), row(ln_a_b[0]), row(onorm_b[0])
    l0_params = (gn0, win0, cw, cb, lng, lnb, lb_logits, og, wout0)

    xp1, conv_prompt, hgrn_prompt = _layer0_prompt(x_prompt, *l0_params)
    xs1, conv_sample, hgrn_sample = _layer0_sample(x_sample, state_conv, state_hgrn, *l0_params)

    gn1 = row(norm_c[0])
    win1 = w_in_c[0].astype(_BF16)
    wout1 = w_out_c[0].astype(_BF16)
    lcg, lcb, fn = row(ln_c_g[0]), row(ln_c_b[0]), row(final_norm)
    reps = CHUNK_C // LS
    wmix_p, bmix_p = w_s[0], b_s[0][:, :, None]
    wmix_s = w_s[0][:, :LS, :]
    bmix_s = jnp.tile(b_s[0][:, :LS], (1, reps))[:, :, None]

    y_prompt, v_prompt = _layer1(xp1.reshape(B * L, D_MODEL), gn1, win1, lcg, lcb, wmix_p, bmix_p, wout1, fn,
                                 R=CHUNK_C, seq_len=L)
    y_sample, v_sample = _layer1(xs1.reshape(NB * LS, D_MODEL), gn1, win1, lcg, lcb, wmix_s, bmix_s, wout1, fn,
                                 R=LS, seq_len=None)

    return (y_prompt.reshape(B, L, D_MODEL), y_sample.reshape(NB, LS, D_MODEL),
            conv_prompt, hgrn_prompt, v_prompt,
            conv_sample, hgrn_sample, v_sample.reshape(1, NB, LS, D_C))
```

```python
import functools

import jax
import jax.numpy as jnp
from jax import lax
from jax.experimental import pallas as pl
from jax.experimental.pallas import tpu as pltpu

D_MODEL = 1024
D_A = D_MODEL
CONV_W = 31
CONV_HIST = CONV_W - 1
H_B = 8
DK_B = 128
DV_B = 128
D_B = H_B * DV_B
HK_B = H_B * DK_B
CHUNK_B = 64
D_C = 2 * D_MODEL
H_C = 8
DG_C = D_C // H_C
CHUNK_C = 128
EPS = 1e-6

SUBLANES = 8
LANES = 128
HIST_ROWS = 32
ROW_TILE = 256
PROMPT_ROW_TILE = 512
L1_PROMPT_ROW_TILE = 512
L1_SAMPLE_ROW_TILE = 512
VMEM_LIMIT_BYTES = 60 * 1024 * 1024

_F32 = jnp.float32
_BF16 = jnp.bfloat16

_NT = (((1,), (1,)), ((), ()))
_TN = (((0,), (0,)), ((), ()))


def _dot(a, b):
    return jnp.dot(a, b, preferred_element_type=_F32)


def _dot_nt(a, b):
    return lax.dot_general(a, b, _NT, preferred_element_type=_F32)


def _dot_tn(a, b):
    return lax.dot_general(a, b, _TN, preferred_element_type=_F32)


def _rms(x, g):
    return x * lax.rsqrt(jnp.mean(x * x, axis=-1, keepdims=True) + EPS) * g


def _layernorm(x, g, b):
    xc = x - jnp.mean(x, axis=-1, keepdims=True)
    return xc * lax.rsqrt(jnp.mean(xc * xc, axis=-1, keepdims=True) + EPS) * g + b


def _silu(x):
    hx = 0.5 * x
    return hx * jnp.tanh(hx) + hx


_GELU_C = 0.7978845608028654
_GELU_C3 = _GELU_C * 0.044715


def _gelu(x):
    hx = 0.5 * x
    return hx * jnp.tanh(x * (_GELU_C3 * (x * x) + _GELU_C)) + hx


def _split3(x):
    hi = x.astype(_BF16)
    r1 = x - hi.astype(_F32)
    mid = r1.astype(_BF16)
    lo = (r1 - mid.astype(_F32)).astype(_BF16)
    return hi, mid, lo


def _dot_exact_lhs(m, x):
    hi, mid, lo = _split3(x)
    return _dot(m, hi) + _dot(m, mid) + _dot(m, lo)


def _forget_lower_bound(lbl_ref):
    l = lbl_ref[...]
    e = jnp.exp(l - jnp.max(l, axis=0, keepdims=True))
    return e[0:1] / jnp.sum(e, axis=0, keepdims=True)


def _l0_prompt_kernel(x_ref, gn_ref, win_ref, cw_ref, cb_ref, lng_ref, lnb_ref, lbl_ref, og_ref, wout_ref,
                      y_ref, conv_ref, hgrn_ref,
                      abuf, st_s, xn_s, yc_s, z_s, lf_s, k_s, v_s, cat_s, *, T):
    t = pl.program_id(1)
    nt = pl.num_programs(1)
    CG = 256
    RBLK = 256
    ANCHOR_SKEW = 6

    @pl.when(t == 0)
    def _():
        abuf[0:HIST_ROWS] = jnp.zeros((HIST_ROWS, D_A), _F32)
        st_s[...] = jnp.zeros_like(st_s)

    off0 = HIST_ROWS - CONV_HIST
    RB = 64
    WIN = RB + HIST_ROWS
    n_lb = D_A // LANES
    z0 = 2 * D_A
    n_dot = (win_ref.shape[1] - z0) // CG
    ZG, ZQ, ZF, ZI, ZB = 0, D_A, D_A + HK_B, D_A + 2 * HK_B, D_A + 2 * HK_B + D_B

    zero_of = lambda v: jnp.minimum(jnp.abs(v), 0.0)

    def conv_block(rb, lb, dep):
        ls = slice(lb * LANES, (lb + 1) * LANES)
        win = abuf[rb * RB:rb * RB + WIN, ls]
        acc = jnp.concatenate([dep, jnp.zeros((RB - SUBLANES, LANES), _F32)], axis=0)
        for s in range(SUBLANES):
            rolled = win if s == 0 else pltpu.roll(win, WIN - s, 0)
            for k in range(CONV_W):
                if (off0 + k) % SUBLANES != s:
                    continue
                base = off0 + k - s
                acc = acc + cw_ref[k:k + 1, ls] * rolled[base:base + RB]
        yc_s[rb * RB:(rb + 1) * RB, ls] = acc
        return zero_of(acc[0:SUBLANES])

    def front(r, dep):
        rs = slice(r * RBLK, (r + 1) * RBLK)
        xn_s[rs] = _rms(x_ref[0, rs], gn_ref[...]).astype(_BF16)
        zdot = lambda c0: _dot(xn_s[rs], win_ref[:, c0:c0 + CG])
        for g in range(D_A // CG):
            a_val = zdot(g * CG)
            a_glu = zdot(D_A + g * CG)
            abuf[HIST_ROWS + r * RBLK:HIST_ROWS + (r + 1) * RBLK, g * CG:(g + 1) * CG] = (
                a_val * jax.nn.sigmoid(a_glu))
        n_conv = (RBLK // RB) * n_lb
        emitted = 0
        dot_deps = {}
        for i in range(n_conv):
            if i - ANCHOR_SKEW in dot_deps:
                dep = dep + dot_deps.pop(i - ANCHOR_SKEW)
            dep = conv_block(r * (RBLK // RB) + i // n_lb, i % n_lb, dep)
            while emitted < ((i + 1) * n_dot) // n_conv:
                z = zdot(z0 + emitted * CG)
                z_s[rs, emitted * CG:(emitted + 1) * CG] = z
                dot_deps[i] = zero_of(z[0:SUBLANES, 0:LANES])
                emitted += 1
        return dep

    C = CHUNK_B
    ri = lax.broadcasted_iota(jnp.int32, (C, C), 0)
    ci = lax.broadcasted_iota(jnp.int32, (C, C), 1)
    tril = (ci <= ri).astype(_BF16)
    rg = lax.broadcasted_iota(jnp.int32, (RBLK, RBLK), 0)
    cg = lax.broadcasted_iota(jnp.int32, (RBLK, RBLK), 1)
    intra = ((rg ^ cg) < C) & (cg <= rg)
    heads = [slice(h * DK_B, (h + 1) * DK_B) for h in range(H_B)]
    lb = _forget_lower_bound(lbl_ref)

    def back(r):
        gs = slice(r * RBLK, (r + 1) * RBLK)
        yc = _layernorm(yc_s[gs] + cb_ref[...], lng_ref[...], lnb_ref[...])
        cat_a = (_silu(yc) * _silu(z_s[gs, ZG:ZG + D_A])).astype(_BF16)
        y_ref[0, gs] = x_ref[0, gs] + _dot(cat_a, wout_ref[0:D_A, :])

        f = lb + (1.0 - lb) * jax.nn.sigmoid(z_s[gs, ZF:ZF + HK_B])
        lf_s[gs] = jnp.log(f)
        k_s[gs] = 1.0 - f
        v_s[gs] = _silu(z_s[gs, ZI:ZI + D_B]).astype(_BF16)

        qa_l, kb_l, qe_l, kc_l, decay_l = [], [], [], [], []
        for c in range(RBLK // C):
            rs = slice(r * RBLK + c * C, r * RBLK + (c + 1) * C)
            bcum = _dot_exact_lhs(tril, lf_s[rs])
            bmid = bcum[C // 2:C // 2 + 1]
            bend = bcum[C - 1:C]
            d = bcum - bmid
            qa = z_s[rs, ZQ:ZQ + HK_B] * jnp.exp(d)
            kb = k_s[rs] * jnp.exp(-d)
            qe_l.append((qa * jnp.exp(bmid)).astype(_BF16))
            kc_l.append((kb * jnp.exp(bend - bmid)).astype(_BF16))
            qa_l.append(qa.astype(_BF16))
            kb_l.append(kb.astype(_BF16))
            decay_l.append(jnp.exp(bend))
        qa = jnp.concatenate(qa_l, axis=0)
        kb = jnp.concatenate(kb_l, axis=0)
        o_intra = []
        for hs in heads:
            sc = jnp.where(intra, _dot_nt(qa[:, hs], kb[:, hs]), 0.0).astype(_BF16)
            o_intra.append(_dot(sc, v_s[gs, hs]))
        for c in range(RBLK // C):
            rs = slice(r * RBLK + c * C, r * RBLK + (c + 1) * C)
            for h, hs in enumerate(heads):
                st = st_s[h]
                o = o_intra[h][c * C:(c + 1) * C] + _dot_nt(qe_l[c][:, hs], st.astype(_BF16))
                st_s[h] = st * decay_l[c][:, hs] + _dot_tn(v_s[rs, hs], kc_l[c][:, hs])
                gate = _silu(z_s[rs, ZB + h * DV_B:ZB + (h + 1) * DV_B])
                cat_s[rs, hs] = (_rms(o, og_ref[:, hs]) * gate).astype(_BF16)
        y_ref[0, gs] += _dot(cat_s[gs], wout_ref[D_A:D_A + D_B, :])

    dep = jnp.zeros((SUBLANES, LANES), _F32)
    dep = front(0, dep)
    for r in range(T // RBLK):
        if r + 1 < T // RBLK:
            dep = front(r + 1, dep)
        back(r)

    abuf[0:HIST_ROWS] = abuf[T:T + HIST_ROWS]

    @pl.when(t == nt - 1)
    def _():
        conv_ref[0, 0] = abuf[T + off0:T + HIST_ROWS]
        for h in range(H_B):
            hgrn_ref[0, 0, h] = st_s[h].T


def _const_spec(shape):
    nd = len(shape)
    return pl.BlockSpec(shape, lambda *_: (0,) * nd, pipeline_mode=pl.Buffered(1))


def _layer0_prompt(x, gn, win, cw, cb, lng, lnb, lbl, og, wout):
    B, L, _ = x.shape
    T = PROMPT_ROW_TILE
    d_in = win.shape[1]
    kern = functools.partial(_l0_prompt_kernel, T=T)
    return pl.pallas_call(
        kern,
        grid=(B, L // T),
        in_specs=[
            pl.BlockSpec((1, T, D_MODEL), lambda b, t: (b, t, 0)),
            _const_spec((1, D_MODEL)),
            _const_spec((D_MODEL, d_in)),
            _const_spec((HIST_ROWS, D_A)),
            _const_spec((1, D_A)),
            _const_spec((1, D_A)),
            _const_spec((1, D_A)),
            _const_spec(lbl.shape),
            _const_spec((1, D_B)),
            _const_spec((D_A + D_B, D_MODEL)),
        ],
        out_specs=[
            pl.BlockSpec((1, T, D_MODEL), lambda b, t: (b, t, 0)),
            pl.BlockSpec((1, 1, CONV_HIST, D_A), lambda b, t: (0, b, 0, 0)),
            pl.BlockSpec((1, 1, H_B, DK_B, DV_B), lambda b, t: (0, b, 0, 0, 0)),
        ],
        out_shape=[
            jax.ShapeDtypeStruct((B, L, D_MODEL), _F32),
            jax.ShapeDtypeStruct((1, B, CONV_HIST, D_A), _F32),
            jax.ShapeDtypeStruct((1, B, H_B, DK_B, DV_B), _F32),
        ],
        scratch_shapes=[
            pltpu.VMEM((T + HIST_ROWS, D_A), _F32),
            pltpu.VMEM((H_B, DV_B, DK_B), _F32),
            pltpu.VMEM((T, D_MODEL), _BF16),
            pltpu.VMEM((T, D_A), _F32),
            pltpu.VMEM((T, d_in - 2 * D_A), _F32),
            pltpu.VMEM((T, HK_B), _F32),
            pltpu.VMEM((T, HK_B), _F32),
            pltpu.VMEM((T, D_B), _BF16),
            pltpu.VMEM((T, D_B), _BF16),
        ],
        compiler_params=pltpu.CompilerParams(
            dimension_semantics=("arbitrary", "arbitrary"),
            vmem_limit_bytes=VMEM_LIMIT_BYTES),
        name="layer0_prompt",
    )(x, gn, win, cw, cb, lng, lnb, lbl, og, wout)


SEQ_PER_STEP = 4
XP_ROWS = 40


def _l0_sample_kernel(x_ref, sc_ref, sh_ref, gn_ref, win_ref, cw_ref, cb_ref, lng_ref, lnb_ref, lbl_ref,
                      og_ref, wout_ref,
                      y_ref, conv_ref, hgrn_ref,
                      xn_s, a_s, xp_s, yc_s, qe_s, kc_s, v_s, lf_s, o_s, cat_s, *, T, LS):
    j = pl.program_id(1)
    nj = pl.num_programs(1)
    G = T // LS

    @pl.when(j == 0)
    def _():
        x = x_ref[...]
        xn_s[...] = _rms(x, gn_ref[...]).astype(_BF16)
        a_val = _dot(xn_s[...], win_ref[:, 0:D_A])
        a_glu = _dot(xn_s[...], win_ref[:, D_A:2 * D_A])
        a_s[...] = a_val * jax.nn.sigmoid(a_glu)

        def conv_seq(g, carry):
            r = pl.multiple_of(g * LS, LS)
            xp_s[0:CONV_HIST] = sc_ref[0, g]
            xp_s[CONV_HIST:CONV_HIST + LS] = a_s[pl.ds(r, LS)]
            conv_ref[0, g] = xp_s[LS:LS + CONV_HIST]
            for lb in range(D_A // LANES):
                ls = slice(lb * LANES, (lb + 1) * LANES)
                win = xp_s[:, ls]
                acc = jnp.zeros((LS, LANES), _F32)
                for s in range(SUBLANES):
                    rolled = win if s == 0 else pltpu.roll(win, XP_ROWS - s, 0)
                    for k in range(CONV_W):
                        if k % SUBLANES != s:
                            continue
                        acc = acc + cw_ref[k:k + 1, ls] * rolled[k - s:k - s + LS]
                yc_s[pl.ds(r, LS), ls] = acc
            return carry

        xp_s[CONV_HIST + LS:XP_ROWS] = jnp.zeros((XP_ROWS - CONV_HIST - LS, D_A), _F32)
        lax.fori_loop(0, G, conv_seq, 0)

        yc = _layernorm(yc_s[...] + cb_ref[...], lng_ref[...], lnb_ref[...])
        a_gate = _dot(xn_s[...], win_ref[:, 2 * D_A:3 * D_A])
        cat_s[:, 0:D_A] = (_silu(yc) * _silu(a_gate)).astype(_BF16)

        c0 = 3 * D_A
        q = _dot(xn_s[...], win_ref[:, c0:c0 + HK_B])
        lb = _forget_lower_bound(lbl_ref)
        f = lb + (1.0 - lb) * jax.nn.sigmoid(_dot(xn_s[...], win_ref[:, c0 + HK_B:c0 + 2 * HK_B]))
        lf_s[...] = jnp.log(f)
        kk = 1.0 - f
        v_s[...] = _silu(_dot(xn_s[...], win_ref[:, c0 + 2 * HK_B:c0 + 2 * HK_B + D_B]))

        ri = lax.broadcasted_iota(jnp.int32, (LANES, LANES), 0)
        ci = lax.broadcasted_iota(jnp.int32, (LANES, LANES), 1)
        same = (ri ^ ci) < LS
        causal = same & (ci <= ri)
        m_cum = causal.astype(_BF16)
        m_mid = (same & ((ci & (LS - 1)) <= LS // 2)).astype(_BF16)
        m_end = same.astype(_BF16)
        for gr in range(T // LANES):
            rs = slice(gr * LANES, (gr + 1) * LANES)
            hi, mid, lo = _split3(lf_s[rs])
            ex = lambda m: _dot(m, hi) + _dot(m, mid) + _dot(m, lo)
            bcum, bmid, bend = ex(m_cum), ex(m_mid), ex(m_end)
            d = bcum - bmid
            qa = q[rs] * jnp.exp(d)
            kb = kk[rs] * jnp.exp(-d)
            qe_s[rs] = qa * jnp.exp(bmid)
            kc_s[rs] = kb * jnp.exp(bend - bmid)
            qa = qa.astype(_BF16)
            kb = kb.astype(_BF16)
            for h in range(H_B):
                hs = slice(h * DK_B, (h + 1) * DK_B)
                sc = jnp.where(causal, _dot_nt(qa[:, hs], kb[:, hs]), 0.0)
                o_s[rs, hs] = _dot(sc.astype(_BF16), v_s[rs, hs].astype(_BF16))

    rhs_sel = jnp.concatenate([jnp.zeros((3 * LS, DV_B), _F32), jnp.ones((3 * LS, DV_B), _F32)], axis=1)
    zpad = jnp.zeros((LS, DV_B), _F32)
    for u in range(SEQ_PER_STEP):
        r = pl.multiple_of((j * SEQ_PER_STEP + u) * LS, LS)
        qe = qe_s[pl.ds(r, LS)].astype(_BF16)
        vv = v_s[pl.ds(r, LS)]
        hi, mid, lo = _split3(lf_s[pl.ds(r, LS)])
        lhs = jnp.concatenate([kc_s[pl.ds(r, LS)], hi.astype(_F32), mid.astype(_F32), lo.astype(_F32)],
                              axis=0).astype(_BF16)
        for h in range(H_B):
            hs = slice(h * DK_B, (h + 1) * DK_B)
            s0 = sh_ref[0, u, h]
            o_s[pl.ds(r, LS), hs] += _dot(qe[:, hs], s0.astype(_BF16))
            rhs = jnp.concatenate([jnp.concatenate([vv[:, hs], zpad], axis=1), rhs_sel], axis=0).astype(_BF16)
            both = _dot_tn(lhs[:, hs], rhs)
            hgrn_ref[0, u, h] = jnp.exp(both[:, DV_B:]) * s0 + both[:, :DV_B]

    @pl.when(j == nj - 1)
    def _():
        c0 = 3 * D_A + 2 * HK_B + D_B
        b_gate = _dot(xn_s[...], win_ref[:, c0:c0 + D_B])
        for h in range(H_B):
            hs = slice(h * DV_B, (h + 1) * DV_B)
            cat_s[:, D_A + h * DV_B:D_A + (h + 1) * DV_B] = (
                _rms(o_s[:, hs], og_ref[:, hs]) * _silu(b_gate[:, hs])).astype(_BF16)
        y_ref[...] = x_ref[...] + _dot(cat_s[...], wout_ref[...])


def _layer0_sample(x, sconv, shgrn, gn, win, cw, cb, lng, lnb, lbl, og, wout):
    NB, LS, _ = x.shape
    T = ROW_TILE
    G = T // LS
    nsteps = G // SEQ_PER_STEP
    d_in = win.shape[1]
    xr = x.reshape(NB * LS, D_MODEL)
    kern = functools.partial(_l0_sample_kernel, T=T, LS=LS)
    y, conv_new, hgrn_new = pl.pallas_call(
        kern,
        grid=(NB // G, nsteps),
        in_specs=[
            pl.BlockSpec((T, D_MODEL), lambda i, j: (i, 0)),
            pl.BlockSpec((1, G, CONV_HIST, D_A), lambda i, j: (0, i, 0, 0)),
            pl.BlockSpec((1, SEQ_PER_STEP, H_B, DK_B, DV_B), lambda i, j: (0, i * nsteps + j, 0, 0, 0)),
            _const_spec((1, D_MODEL)),
            _const_spec((D_MODEL, d_in)),
            _const_spec((HIST_ROWS, D_A)),
            _const_spec((1, D_A)),
            _const_spec((1, D_A)),
            _const_spec((1, D_A)),
            _const_spec(lbl.shape),
            _const_spec((1, D_B)),
            _const_spec((D_A + D_B, D_MODEL)),
        ],
        out_specs=[
            pl.BlockSpec((T, D_MODEL), lambda i, j: (i, 0)),
            pl.BlockSpec((1, G, CONV_HIST, D_A), lambda i, j: (0, i, 0, 0)),
            pl.BlockSpec((1, SEQ_PER_STEP, H_B, DK_B, DV_B), lambda i, j: (0, i * nsteps + j, 0, 0, 0)),
        ],
        out_shape=[
            jax.ShapeDtypeStruct((NB * LS, D_MODEL), _F32),
            jax.ShapeDtypeStruct(sconv.shape, _F32),
            jax.ShapeDtypeStruct(shgrn.shape, _F32),
        ],
        scratch_shapes=[
            pltpu.VMEM((T, D_MODEL), _BF16),
            pltpu.VMEM((T, D_A), _F32),
            pltpu.VMEM((XP_ROWS, D_A), _F32),
            pltpu.VMEM((T, D_A), _F32),
            pltpu.VMEM((T, HK_B), _F32),
            pltpu.VMEM((T, HK_B), _F32),
            pltpu.VMEM((T, D_B), _F32),
            pltpu.VMEM((T, HK_B), _F32),
            pltpu.VMEM((T, D_B), _F32),
            pltpu.VMEM((T, D_A + D_B), _BF16),
        ],
        compiler_params=pltpu.CompilerParams(
            dimension_semantics=("arbitrary", "arbitrary"),
            vmem_limit_bytes=VMEM_LIMIT_BYTES),
        name="layer0_sample",
    )(xr, sconv, shgrn, gn, win, cw, cb, lng, lnb, lbl, og, wout)
    return y.reshape(NB, LS, D_MODEL), conv_new, hgrn_new


def _l1_kernel(x_ref, gn_ref, win_ref, lng_ref, lnb_ref, wmix_ref, bmix_ref, wout_ref, fn_ref,
               y_ref, v_ref, xn_s, z_s, vb_s, ug_s, p_s, *rest, T, R, tiles_per_seq):
    i = pl.program_id(0)
    RBLK = min(T, 256)
    nblk = T // RBLK
    zero_of = lambda t: jnp.minimum(jnp.abs(t), 0.0)

    ri = lax.broadcasted_iota(jnp.int32, (CHUNK_C, CHUNK_C), 0)
    ci = lax.broadcasted_iota(jnp.int32, (CHUNK_C, CHUNK_C), 1)
    mask = ((ri ^ ci) < R) & (ci <= ri)
    wms = []
    for h in range(H_C):
        wg = wmix_ref[h]
        if R < CHUNK_C:
            lane = lax.broadcasted_iota(jnp.int32, wg.shape, 1)
            wg = jnp.where(lane < R, wg, 0.0)
            span = R
            while span < CHUNK_C:
                wg = wg + pltpu.roll(wg, span, 1)
                span *= 2
            wg = jnp.concatenate([wg] * (CHUNK_C // R), axis=0)
        wms.append(jnp.where(mask, wg, 0.0).astype(_BF16))

    slot = lambda r: slice((r % 2) * RBLK, (r % 2 + 1) * RBLK)
    CG = DG_C
    n_dot = 3 * D_C // CG
    PIECE_SKEW = 4

    def norm_rows(r):
        rows = slice(r * RBLK, (r + 1) * RBLK)
        xn_s[slot(r)] = _rms(x_ref[rows], gn_ref[...]).astype(_BF16)

    def proj_group(r, k):
        z = _dot(xn_s[slot(r)], win_ref[:, k * CG:(k + 1) * CG])
        z_s[slot(r), k * CG:(k + 1) * CG] = z
        return zero_of(z[0:SUBLANES, 0:LANES])

    def with_dep(val, dep):
        top = jnp.concatenate([val[0:SUBLANES, 0:LANES] + dep, val[0:SUBLANES, LANES:]], axis=1)
        return jnp.concatenate([top, val[SUBLANES:]], axis=0)

    def back(r, dep, nxt):
        rows = slice(r * RBLK, (r + 1) * RBLK)
        sl = slot(r)
        if nxt is not None:
            norm_rows(nxt)
        n_piece = 2 * H_C
        emitted = 0
        dot_deps = {}
        s1 = jnp.zeros((RBLK, 1), _F32)
        for pi in range(n_piece):
            if pi - PIECE_SKEW in dot_deps:
                dep = dep + dot_deps.pop(pi - PIECE_SKEW)
            g = pi // 2
            cs = slice(g * CG, (g + 1) * CG)
            if pi % 2 == 0:
                zu = with_dep(z_s[sl, g * CG:(g + 1) * CG], dep)
                ug = _gelu(zu) * _silu(z_s[sl, 2 * D_C + g * CG:2 * D_C + (g + 1) * CG])
                ug_s[sl, cs] = ug
                dep = zero_of(ug[0:SUBLANES, 0:LANES])
            else:
                gv = _gelu(with_dep(z_s[sl, D_C + g * CG:D_C + (g + 1) * CG], dep))
                z_s[sl, D_C + g * CG:D_C + (g + 1) * CG] = gv
                s1 = s1 + jnp.sum(gv, axis=-1, keepdims=True)
                dep = zero_of(gv[0:SUBLANES, 0:LANES])
            if nxt is not None:
                while emitted < ((pi + 1) * n_dot) // n_piece:
                    d = proj_group(nxt, emitted)
                    dot_deps[pi] = d if pi not in dot_deps else dot_deps[pi] + d
                    emitted += 1
        for d in dot_deps.values():
            dep = dep + d
        c = z_s[sl, D_C:2 * D_C] - s1 * (1.0 / D_C)
        v = c * lax.rsqrt(jnp.mean(c * c, axis=-1, keepdims=True) + EPS) * lng_ref[...] + lnb_ref[...]
        vb_s[sl] = v.astype(_BF16)
        if tiles_per_seq is None:
            v_ref[rows] = v
        elif r == nblk - 1:
            rest[0][...] = v[RBLK - CHUNK_C:RBLK]
        for h in range(H_C):
            hs = slice(h * DG_C, (h + 1) * DG_C)
            for gr in range(RBLK // CHUNK_C):
                rs = slice(sl.start + gr * CHUNK_C, sl.start + (gr + 1) * CHUNK_C)
                mix = _dot(wms[h], vb_s[rs, hs]) + bmix_ref[h]
                p_s[rs, hs] = (ug_s[rs, hs] * mix).astype(_BF16)
        y = x_ref[rows] + _dot(p_s[sl], wout_ref[...])
        y_ref[rows] = _rms(y, fn_ref[...])
        return dep + zero_of(y[0:SUBLANES, 0:LANES])

    norm_rows(0)
    for k in range(n_dot):
        proj_group(0, k)
    dep = jnp.zeros((SUBLANES, LANES), _F32)
    for r in range(nblk):
        dep = back(r, dep, r + 1 if r + 1 < nblk else None)

    if tiles_per_seq is not None:
        @pl.when(i % tiles_per_seq == tiles_per_seq - 1)
        def _():
            v_ref[0, 0] = rest[0][...]


def _layer1(xr, gn, win, lng, lnb, wmix, bmix, wout, fn, *, R, seq_len):
    N = xr.shape[0]
    T = L1_SAMPLE_ROW_TILE if seq_len is None else L1_PROMPT_ROW_TILE
    srows = min(T, 2 * 256)
    if seq_len is None:
        tiles_per_seq = None
        v_shape = (N, D_C)
        v_spec = pl.BlockSpec((T, D_C), lambda i: (i, 0))
    else:
        tiles_per_seq = seq_len // T
        v_shape = (1, N // seq_len, CHUNK_C, D_C)
        v_spec = pl.BlockSpec((1, 1, CHUNK_C, D_C), lambda i: (0, i // tiles_per_seq, 0, 0))
    kern = functools.partial(_l1_kernel, T=T, R=R, tiles_per_seq=tiles_per_seq)
    return pl.pallas_call(
        kern,
        grid=(N // T,),
        in_specs=[
            pl.BlockSpec((T, D_MODEL), lambda i: (i, 0)),
            _const_spec((1, D_MODEL)),
            _const_spec((D_MODEL, 3 * D_C)),
            _const_spec((1, D_C)),
            _const_spec((1, D_C)),
            _const_spec(wmix.shape),
            _const_spec((H_C, CHUNK_C, 1)),
            _const_spec((D_C, D_MODEL)),
            _const_spec((1, D_MODEL)),
        ],
        out_specs=[pl.BlockSpec((T, D_MODEL), lambda i: (i, 0)), v_spec],
        out_shape=[jax.ShapeDtypeStruct((N, D_MODEL), _F32), jax.ShapeDtypeStruct(v_shape, _F32)],
        scratch_shapes=[
            pltpu.VMEM((srows, D_MODEL), _BF16),
            pltpu.VMEM((srows, 3 * D_C), _F32),
            pltpu.VMEM((srows, D_C), _BF16),
            pltpu.VMEM((srows, D_C), _F32),
            pltpu.VMEM((srows, D_C), _BF16),
        ] + ([] if seq_len is None else [pltpu.VMEM((CHUNK_C, D_C), _F32)]),
        compiler_params=pltpu.CompilerParams(
            dimension_semantics=("arbitrary",),
            vmem_limit_bytes=VMEM_LIMIT_BYTES),
        name="layer1_prompt" if seq_len is not None else "layer1_sample",
    )(xr, gn, win, lng, lnb, wmix, bmix, wout, fn)


def kernel(x_prompt, x_sample, state_conv, state_hgrn, norm_ab, w_in_ab, conv_w, conv_b, ln_a_g, ln_a_b, lb_logits, onorm_b, w_out_ab, norm_c, w_in_c, ln_c_g, ln_c_b, w_s, b_s, w_out_c, final_norm):
    B, L, _ = x_prompt.shape
    NB, LS, _ = x_sample.shape
    row = lambda p: p.reshape(1, -1)

    gn0 = row(norm_ab[0])
    win0 = w_in_ab[0].astype(_BF16)
    wout0 = w_out_ab[0].astype(_BF16)
    cw = jnp.pad(conv_w[0], ((0, HIST_ROWS - CONV_W), (0, 0)))
    cb, lng, lnb, og = row(conv_b[0]), row(ln_a_g[0]), row(ln_a_b[0]), row(onorm_b[0])
    l0_params = (gn0, win0, cw, cb, lng, lnb, lb_logits, og, wout0)

    xp1, conv_prompt, hgrn_prompt = _layer0_prompt(x_prompt, *l0_params)
    xs1, conv_sample, hgrn_sample = _layer0_sample(x_sample, state_conv, state_hgrn, *l0_params)

    gn1 = row(norm_c[0])
    win1 = w_in_c[0].astype(_BF16)
    wout1 = w_out_c[0].astype(_BF16)
    lcg, lcb, fn = row(ln_c_g[0]), row(ln_c_b[0]), row(final_norm)
    reps = CHUNK_C // LS
    wmix_p, bmix_p = w_s[0], b_s[0][:, :, None]
    wmix_s = w_s[0][:, :LS, :]
    bmix_s = jnp.tile(b_s[0][:, :LS], (1, reps))[:, :, None]

    y_prompt, v_prompt = _layer1(xp1.reshape(B * L, D_MODEL), gn1, win1, lcg, lcb, wmix_p, bmix_p, wout1, fn,
                                 R=CHUNK_C, seq_len=L)
    y_sample, v_sample = _layer1(xs1.reshape(NB * LS, D_MODEL), gn1, win1, lcg, lcb, wmix_s, bmix_s, wout1, fn,
                                 R=LS, seq_len=None)

    return (y_prompt.reshape(B, L, D_MODEL), y_sample.reshape(NB, LS, D_MODEL),
            conv_prompt, hgrn_prompt, v_prompt,
            conv_sample, hgrn_sample, v_sample.reshape(1, NB, LS, D_C))
```

```python
import functools

import jax
import jax.numpy as jnp
from jax import lax
from jax.experimental import pallas as pl
from jax.experimental.pallas import tpu as pltpu

D_MODEL = 1024
D_A = D_MODEL
CONV_W = 31
CONV_HIST = CONV_W - 1
H_B = 8
DK_B = 128
DV_B = 128
D_B = H_B * DV_B
HK_B = H_B * DK_B
CHUNK_B = 64
D_C = 2 * D_MODEL
H_C = 8
DG_C = D_C // H_C
CHUNK_C = 128
EPS = 1e-6

SUBLANES = 8
LANES = 128
HIST_ROWS = 32
ROW_TILE = 256
PROMPT_ROW_TILE = 512
L1_PROMPT_ROW_TILE = 512
L1_SAMPLE_ROW_TILE = 512
VMEM_LIMIT_BYTES = 60 * 1024 * 1024

_F32 = jnp.float32
_BF16 = jnp.bfloat16

_NT = (((1,), (1,)), ((), ()))
_TN = (((0,), (0,)), ((), ()))


def _dot(a, b):
    return jnp.dot(a, b, preferred_element_type=_F32)


def _dot_nt(a, b):
    return lax.dot_general(a, b, _NT, preferred_element_type=_F32)


def _dot_tn(a, b):
    return lax.dot_general(a, b, _TN, preferred_element_type=_F32)


def _rms(x, g):
    return x * lax.rsqrt(jnp.mean(x * x, axis=-1, keepdims=True) + EPS) * g


def _layernorm(x, g, b):
    xc = x - jnp.mean(x, axis=-1, keepdims=True)
    return xc * lax.rsqrt(jnp.mean(xc * xc, axis=-1, keepdims=True) + EPS) * g + b


def _silu(x):
    hx = 0.5 * x
    return hx * jnp.tanh(hx) + hx


_GELU_C = 0.7978845608028654
_GELU_C3 = _GELU_C * 0.044715


def _gelu(x):
    hx = 0.5 * x
    return hx * jnp.tanh(x * (_GELU_C3 * (x * x) + _GELU_C)) + hx


def _split3(x):
    hi = x.astype(_BF16)
    r1 = x - hi.astype(_F32)
    mid = r1.astype(_BF16)
    lo = (r1 - mid.astype(_F32)).astype(_BF16)
    return hi, mid, lo


def _dot_exact_lhs(m, x):
    hi, mid, lo = _split3(x)
    return _dot(m, hi) + _dot(m, mid) + _dot(m, lo)


def _forget_lower_bound(lbl_ref):
    l = lbl_ref[...]
    e = jnp.exp(l - jnp.max(l, axis=0, keepdims=True))
    return e[0:1] / jnp.sum(e, axis=0, keepdims=True)


def _l0_prompt_kernel(x_ref, gn_ref, win_ref, cw_ref, cb_ref, lng_ref, lnb_ref, lbl_ref, og_ref, wout_ref,
                      y_ref, conv_ref, hgrn_ref,
                      abuf, st_s, xn_s, yc_s, z_s, lf_s, k_s, v_s, cat_s, *, T):
    t = pl.program_id(1)
    nt = pl.num_programs(1)
    CG = 256
    RBLK = 256
    ANCHOR_SKEW = 4

    @pl.when(t == 0)
    def _():
        abuf[0:HIST_ROWS] = jnp.zeros((HIST_ROWS, D_A), _F32)
        st_s[...] = jnp.zeros_like(st_s)

    off0 = HIST_ROWS - CONV_HIST
    RB = 64
    WIN = RB + HIST_ROWS
    n_lb = D_A // LANES
    z0 = 2 * D_A
    n_dot = (win_ref.shape[1] - z0) // CG
    ZG, ZQ, ZF, ZI, ZB = 0, D_A, D_A + HK_B, D_A + 2 * HK_B, D_A + 2 * HK_B + D_B

    zero_of = lambda v: jnp.minimum(jnp.abs(v), 0.0)

    def conv_block(rb, lb, dep):
        ls = slice(lb * LANES, (lb + 1) * LANES)
        win = abuf[rb * RB:rb * RB + WIN, ls]
        acc = jnp.concatenate([dep, jnp.zeros((RB - SUBLANES, LANES), _F32)], axis=0)
        for s in range(SUBLANES):
            rolled = win if s == 0 else pltpu.roll(win, WIN - s, 0)
            for k in range(CONV_W):
                if (off0 + k) % SUBLANES != s:
                    continue
                base = off0 + k - s
                acc = acc + cw_ref[k:k + 1, ls] * rolled[base:base + RB]
        yc_s[rb * RB:(rb + 1) * RB, ls] = acc
        return zero_of(acc[0:SUBLANES])

    def front(r, dep):
        rs = slice(r * RBLK, (r + 1) * RBLK)
        xn_s[rs] = _rms(x_ref[0, rs], gn_ref[...]).astype(_BF16)
        zdot = lambda c0: _dot(xn_s[rs], win_ref[:, c0:c0 + CG])
        for g in range(D_A // CG):
            a_val = zdot(g * CG)
            a_glu = zdot(D_A + g * CG)
            abuf[HIST_ROWS + r * RBLK:HIST_ROWS + (r + 1) * RBLK, g * CG:(g + 1) * CG] = (
                a_val * jax.nn.sigmoid(a_glu))
        n_conv = (RBLK // RB) * n_lb
        emitted = 0
        dot_deps = {}
        for i in range(n_conv):
            if i - ANCHOR_SKEW in dot_deps:
                dep = dep + dot_deps.pop(i - ANCHOR_SKEW)
            dep = conv_block(r * (RBLK // RB) + i // n_lb, i % n_lb, dep)
            while emitted < ((i + 1) * n_dot) // n_conv:
                z = zdot(z0 + emitted * CG)
                z_s[rs, emitted * CG:(emitted + 1) * CG] = z
                dot_deps[i] = zero_of(z[0:SUBLANES, 0:LANES])
                emitted += 1
        return dep

    C = CHUNK_B
    ri = lax.broadcasted_iota(jnp.int32, (C, C), 0)
    ci = lax.broadcasted_iota(jnp.int32, (C, C), 1)
    tril = (ci <= ri).astype(_BF16)
    rg = lax.broadcasted_iota(jnp.int32, (RBLK, RBLK), 0)
    cg = lax.broadcasted_iota(jnp.int32, (RBLK, RBLK), 1)
    intra = ((rg ^ cg) < C) & (cg <= rg)
    heads = [slice(h * DK_B, (h + 1) * DK_B) for h in range(H_B)]
    lb = _forget_lower_bound(lbl_ref)

    def back(r):
        gs = slice(r * RBLK, (r + 1) * RBLK)
        yc = _layernorm(yc_s[gs] + cb_ref[...], lng_ref[...], lnb_ref[...])
        cat_a = (_silu(yc) * _silu(z_s[gs, ZG:ZG + D_A])).astype(_BF16)
        y_ref[0, gs] = x_ref[0, gs] + _dot(cat_a, wout_ref[0:D_A, :])

        f = lb + (1.0 - lb) * jax.nn.sigmoid(z_s[gs, ZF:ZF + HK_B])
        lf_s[gs] = jnp.log(f)
        k_s[gs] = 1.0 - f
        v_s[gs] = _silu(z_s[gs, ZI:ZI + D_B]).astype(_BF16)

        qa_l, kb_l, qe_l, kc_l, decay_l = [], [], [], [], []
        for c in range(RBLK // C):
            rs = slice(r * RBLK + c * C, r * RBLK + (c + 1) * C)
            bcum = _dot_exact_lhs(tril, lf_s[rs])
            bmid = bcum[C // 2:C // 2 + 1]
            bend = bcum[C - 1:C]
            d = bcum - bmid
            qa = z_s[rs, ZQ:ZQ + HK_B] * jnp.exp(d)
            kb = k_s[rs] * jnp.exp(-d)
            qe_l.append((qa * jnp.exp(bmid)).astype(_BF16))
            kc_l.append((kb * jnp.exp(bend - bmid)).astype(_BF16))
            qa_l.append(qa.astype(_BF16))
            kb_l.append(kb.astype(_BF16))
            decay_l.append(jnp.exp(bend))
        qa = jnp.concatenate(qa_l, axis=0)
        kb = jnp.concatenate(kb_l, axis=0)
        o_intra = []
        for hs in heads:
            sc = jnp.where(intra, _dot_nt(qa[:, hs], kb[:, hs]), 0.0).astype(_BF16)
            o_intra.append(_dot(sc, v_s[gs, hs]))
        for c in range(RBLK // C):
            rs = slice(r * RBLK + c * C, r * RBLK + (c + 1) * C)
            for h, hs in enumerate(heads):
                st = st_s[h]
                o = o_intra[h][c * C:(c + 1) * C] + _dot_nt(qe_l[c][:, hs], st.astype(_BF16))
                st_s[h] = st * decay_l[c][:, hs] + _dot_tn(v_s[rs, hs], kc_l[c][:, hs])
                gate = _silu(z_s[rs, ZB + h * DV_B:ZB + (h + 1) * DV_B])
                cat_s[rs, hs] = (_rms(o, og_ref[:, hs]) * gate).astype(_BF16)
        y_ref[0, gs] += _dot(cat_s[gs], wout_ref[D_A:D_A + D_B, :])

    dep = jnp.zeros((SUBLANES, LANES), _F32)
    dep = front(0, dep)
    for r in range(T // RBLK):
        if r + 1 < T // RBLK:
            dep = front(r + 1, dep)
        back(r)

    abuf[0:HIST_ROWS] = abuf[T:T + HIST_ROWS]

    @pl.when(t == nt - 1)
    def _():
        conv_ref[0, 0] = abuf[T + off0:T + HIST_ROWS]
        for h in range(H_B):
            hgrn_ref[0, 0, h] = st_s[h].T


def _const_spec(shape):
    nd = len(shape)
    return pl.BlockSpec(shape, lambda *_: (0,) * nd, pipeline_mode=pl.Buffered(1))


def _layer0_prompt(x, gn, win, cw, cb, lng, lnb, lbl, og, wout):
    B, L, _ = x.shape
    T = PROMPT_ROW_TILE
    d_in = win.shape[1]
    kern = functools.partial(_l0_prompt_kernel, T=T)
    return pl.pallas_call(
        kern,
        grid=(B, L // T),
        in_specs=[
            pl.BlockSpec((1, T, D_MODEL), lambda b, t: (b, t, 0)),
            _const_spec((1, D_MODEL)),
            _const_spec((D_MODEL, d_in)),
            _const_spec((HIST_ROWS, D_A)),
            _const_spec((1, D_A)),
            _const_spec((1, D_A)),
            _const_spec((1, D_A)),
            _const_spec(lbl.shape),
            _const_spec((1, D_B)),
            _const_spec((D_A + D_B, D_MODEL)),
        ],
        out_specs=[
            pl.BlockSpec((1, T, D_MODEL), lambda b, t: (b, t, 0)),
            pl.BlockSpec((1, 1, CONV_HIST, D_A), lambda b, t: (0, b, 0, 0)),
            pl.BlockSpec((1, 1, H_B, DK_B, DV_B), lambda b, t: (0, b, 0, 0, 0)),
        ],
        out_shape=[
            jax.ShapeDtypeStruct((B, L, D_MODEL), _F32),
            jax.ShapeDtypeStruct((1, B, CONV_HIST, D_A), _F32),
            jax.ShapeDtypeStruct((1, B, H_B, DK_B, DV_B), _F32),
        ],
        scratch_shapes=[
            pltpu.VMEM((T + HIST_ROWS, D_A), _F32),
            pltpu.VMEM((H_B, DV_B, DK_B), _F32),
            pltpu.VMEM((T, D_MODEL), _BF16),
            pltpu.VMEM((T, D_A), _F32),
            pltpu.VMEM((T, d_in - 2 * D_A), _F32),
            pltpu.VMEM((T, HK_B), _F32),
            pltpu.VMEM((T, HK_B), _F32),
            pltpu.VMEM((T, D_B), _BF16),
            pltpu.VMEM((T, D_B), _BF16),
        ],
        compiler_params=pltpu.CompilerParams(
            dimension_semantics=("arbitrary", "arbitrary"),
            vmem_limit_bytes=VMEM_LIMIT_BYTES),
        name="layer0_prompt",
    )(x, gn, win, cw, cb, lng, lnb, lbl, og, wout)


SEQ_PER_STEP = 4
XP_ROWS = 40


def _l0_sample_kernel(x_ref, sc_ref, sh_ref, gn_ref, win_ref, cw_ref, cb_ref, lng_ref, lnb_ref, lbl_ref,
                      og_ref, wout_ref,
                      y_ref, conv_ref, hgrn_ref,
                      xn_s, a_s, xp_s, yc_s, qe_s, kc_s, v_s, lf_s, o_s, cat_s, *, T, LS):
    j = pl.program_id(1)
    nj = pl.num_programs(1)
    G = T // LS

    @pl.when(j == 0)
    def _():
        x = x_ref[...]
        xn_s[...] = _rms(x, gn_ref[...]).astype(_BF16)
        a_val = _dot(xn_s[...], win_ref[:, 0:D_A])
        a_glu = _dot(xn_s[...], win_ref[:, D_A:2 * D_A])
        a_s[...] = a_val * jax.nn.sigmoid(a_glu)

        def conv_seq(g, carry):
            r = pl.multiple_of(g * LS, LS)
            xp_s[0:CONV_HIST] = sc_ref[0, g]
            xp_s[CONV_HIST:CONV_HIST + LS] = a_s[pl.ds(r, LS)]
            conv_ref[0, g] = xp_s[LS:LS + CONV_HIST]
            for lb in range(D_A // LANES):
                ls = slice(lb * LANES, (lb + 1) * LANES)
                win = xp_s[:, ls]
                acc = jnp.zeros((LS, LANES), _F32)
                for s in range(SUBLANES):
                    rolled = win if s == 0 else pltpu.roll(win, XP_ROWS - s, 0)
                    for k in range(CONV_W):
                        if k % SUBLANES != s:
                            continue
                        acc = acc + cw_ref[k:k + 1, ls] * rolled[k - s:k - s + LS]
                yc_s[pl.ds(r, LS), ls] = acc
            return carry

        xp_s[CONV_HIST + LS:XP_ROWS] = jnp.zeros((XP_ROWS - CONV_HIST - LS, D_A), _F32)
        lax.fori_loop(0, G, conv_seq, 0)

        yc = _layernorm(yc_s[...] + cb_ref[...], lng_ref[...], lnb_ref[...])
        a_gate = _dot(xn_s[...], win_ref[:, 2 * D_A:3 * D_A])
        cat_s[:, 0:D_A] = (_silu(yc) * _silu(a_gate)).astype(_BF16)

        c0 = 3 * D_A
        q = _dot(xn_s[...], win_ref[:, c0:c0 + HK_B])
        lb = _forget_lower_bound(lbl_ref)
        f = lb + (1.0 - lb) * jax.nn.sigmoid(_dot(xn_s[...], win_ref[:, c0 + HK_B:c0 + 2 * HK_B]))
        lf_s[...] = jnp.log(f)
        kk = 1.0 - f
        v_s[...] = _silu(_dot(xn_s[...], win_ref[:, c0 + 2 * HK_B:c0 + 2 * HK_B + D_B]))

        ri = lax.broadcasted_iota(jnp.int32, (LANES, LANES), 0)
        ci = lax.broadcasted_iota(jnp.int32, (LANES, LANES), 1)
        same = (ri ^ ci) < LS
        causal = same & (ci <= ri)
        m_cum = causal.astype(_BF16)
        m_mid = (same & ((ci & (LS - 1)) <= LS // 2)).astype(_BF16)
        m_end = same.astype(_BF16)
        for gr in range(T // LANES):
            rs = slice(gr * LANES, (gr + 1) * LANES)
            hi, mid, lo = _split3(lf_s[rs])
            ex = lambda m: _dot(m, hi) + _dot(m, mid) + _dot(m, lo)
            bcum, bmid, bend = ex(m_cum), ex(m_mid), ex(m_end)
            d = bcum - bmid
            qa = q[rs] * jnp.exp(d)
            kb = kk[rs] * jnp.exp(-d)
            qe_s[rs] = qa * jnp.exp(bmid)
            kc_s[rs] = kb * jnp.exp(bend - bmid)
            qa = qa.astype(_BF16)
            kb = kb.astype(_BF16)
            for h in range(H_B):
                hs = slice(h * DK_B, (h + 1) * DK_B)
                sc = jnp.where(causal, _dot_nt(qa[:, hs], kb[:, hs]), 0.0)
                o_s[rs, hs] = _dot(sc.astype(_BF16), v_s[rs, hs].astype(_BF16))

    rhs_sel = jnp.concatenate([jnp.zeros((3 * LS, DV_B), _F32), jnp.ones((3 * LS, DV_B), _F32)], axis=1)
    zpad = jnp.zeros((LS, DV_B), _F32)
    for u in range(SEQ_PER_STEP):
        r = pl.multiple_of((j * SEQ_PER_STEP + u) * LS, LS)
        qe = qe_s[pl.ds(r, LS)].astype(_BF16)
        vv = v_s[pl.ds(r, LS)]
        hi, mid, lo = _split3(lf_s[pl.ds(r, LS)])
        lhs = jnp.concatenate([kc_s[pl.ds(r, LS)], hi.astype(_F32), mid.astype(_F32), lo.astype(_F32)],
                              axis=0).astype(_BF16)
        for h in range(H_B):
            hs = slice(h * DK_B, (h + 1) * DK_B)
            s0 = sh_ref[0, u, h]
            o_s[pl.ds(r, LS), hs] += _dot(qe[:, hs], s0.astype(_BF16))
            rhs = jnp.concatenate([jnp.concatenate([vv[:, hs], zpad], axis=1), rhs_sel], axis=0).astype(_BF16)
            both = _dot_tn(lhs[:, hs], rhs)
            hgrn_ref[0, u, h] = jnp.exp(both[:, DV_B:]) * s0 + both[:, :DV_B]

    @pl.when(j == nj - 1)
    def _():
        c0 = 3 * D_A + 2 * HK_B + D_B
        b_gate = _dot(xn_s[...], win_ref[:, c0:c0 + D_B])
        for h in range(H_B):
            hs = slice(h * DV_B, (h + 1) * DV_B)
            cat_s[:, D_A + h * DV_B:D_A + (h + 1) * DV_B] = (
                _rms(o_s[:, hs], og_ref[:, hs]) * _silu(b_gate[:, hs])).astype(_BF16)
        y_ref[...] = x_ref[...] + _dot(cat_s[...], wout_ref[...])


def _layer0_sample(x, sconv, shgrn, gn, win, cw, cb, lng, lnb, lbl, og, wout):
    NB, LS, _ = x.shape
    T = ROW_TILE
    G = T // LS
    nsteps = G // SEQ_PER_STEP
    d_in = win.shape[1]
    xr = x.reshape(NB * LS, D_MODEL)
    kern = functools.partial(_l0_sample_kernel, T=T, LS=LS)
    y, conv_new, hgrn_new = pl.pallas_call(
        kern,
        grid=(NB // G, nsteps),
        in_specs=[
            pl.BlockSpec((T, D_MODEL), lambda i, j: (i, 0)),
            pl.BlockSpec((1, G, CONV_HIST, D_A), lambda i, j: (0, i, 0, 0)),
            pl.BlockSpec((1, SEQ_PER_STEP, H_B, DK_B, DV_B), lambda i, j: (0, i * nsteps + j, 0, 0, 0)),
            _const_spec((1, D_MODEL)),
            _const_spec((D_MODEL, d_in)),
            _const_spec((HIST_ROWS, D_A)),
            _const_spec((1, D_A)),
            _const_spec((1, D_A)),
            _const_spec((1, D_A)),
            _const_spec(lbl.shape),
            _const_spec((1, D_B)),
            _const_spec((D_A + D_B, D_MODEL)),
        ],
        out_specs=[
            pl.BlockSpec((T, D_MODEL), lambda i, j: (i, 0)),
            pl.BlockSpec((1, G, CONV_HIST, D_A), lambda i, j: (0, i, 0, 0)),
            pl.BlockSpec((1, SEQ_PER_STEP, H_B, DK_B, DV_B), lambda i, j: (0, i * nsteps + j, 0, 0, 0)),
        ],
        out_shape=[
            jax.ShapeDtypeStruct((NB * LS, D_MODEL), _F32),
            jax.ShapeDtypeStruct(sconv.shape, _F32),
            jax.ShapeDtypeStruct(shgrn.shape, _F32),
        ],
        scratch_shapes=[
            pltpu.VMEM((T, D_MODEL), _BF16),
            pltpu.VMEM((T, D_A), _F32),
            pltpu.VMEM((XP_ROWS, D_A), _F32),
            pltpu.VMEM((T, D_A), _F32),
            pltpu.VMEM((T, HK_B), _F32),
            pltpu.VMEM((T, HK_B), _F32),
            pltpu.VMEM((T, D_B), _F32),
            pltpu.VMEM((T, HK_B), _F32),
            pltpu.VMEM((T, D_B), _F32),
            pltpu.VMEM((T, D_A + D_B), _BF16),
        ],
        compiler_params=pltpu.CompilerParams(
            dimension_semantics=("arbitrary", "arbitrary"),
            vmem_limit_bytes=VMEM_LIMIT_BYTES),
        name="layer0_sample",
    )(xr, sconv, shgrn, gn, win, cw, cb, lng, lnb, lbl, og, wout)
    return y.reshape(NB, LS, D_MODEL), conv_new, hgrn_new


def _l1_kernel(x_ref, gn_ref, win_ref, lng_ref, lnb_ref, wmix_ref, bmix_ref, wout_ref, fn_ref,
               y_ref, v_ref, xn_s, z_s, vb_s, ug_s, p_s, *rest, T, R, tiles_per_seq):
    i = pl.program_id(0)
    RBLK = min(T, 256)
    nblk = T // RBLK
    zero_of = lambda t: jnp.minimum(jnp.abs(t), 0.0)

    ri = lax.broadcasted_iota(jnp.int32, (CHUNK_C, CHUNK_C), 0)
    ci = lax.broadcasted_iota(jnp.int32, (CHUNK_C, CHUNK_C), 1)
    mask = ((ri ^ ci) < R) & (ci <= ri)
    wms = []
    for h in range(H_C):
        wg = wmix_ref[h]
        if R < CHUNK_C:
            lane = lax.broadcasted_iota(jnp.int32, wg.shape, 1)
            wg = jnp.where(lane < R, wg, 0.0)
            span = R
            while span < CHUNK_C:
                wg = wg + pltpu.roll(wg, span, 1)
                span *= 2
            wg = jnp.concatenate([wg] * (CHUNK_C // R), axis=0)
        wms.append(jnp.where(mask, wg, 0.0).astype(_BF16))

    slot = lambda r: slice((r % 2) * RBLK, (r % 2 + 1) * RBLK)
    CG = DG_C
    n_dot = 3 * D_C // CG
    PIECE_SKEW = 6

    def norm_rows(r):
        rows = slice(r * RBLK, (r + 1) * RBLK)
        xn_s[slot(r)] = _rms(x_ref[rows], gn_ref[...]).astype(_BF16)

    def proj_group(r, k):
        z = _dot(xn_s[slot(r)], win_ref[:, k * CG:(k + 1) * CG])
        z_s[slot(r), k * CG:(k + 1) * CG] = z
        return zero_of(z[0:SUBLANES, 0:LANES])

    def with_dep(val, dep):
        top = jnp.concatenate([val[0:SUBLANES, 0:LANES] + dep, val[0:SUBLANES, LANES:]], axis=1)
        return jnp.concatenate([top, val[SUBLANES:]], axis=0)

    def back(r, dep, nxt):
        rows = slice(r * RBLK, (r + 1) * RBLK)
        sl = slot(r)
        if nxt is not None:
            norm_rows(nxt)
        n_piece = 2 * H_C
        emitted = 0
        dot_deps = {}
        s1 = jnp.zeros((RBLK, 1), _F32)
        for pi in range(n_piece):
            if pi - PIECE_SKEW in dot_deps:
                dep = dep + dot_deps.pop(pi - PIECE_SKEW)
            g = pi // 2
            cs = slice(g * CG, (g + 1) * CG)
            if pi % 2 == 0:
                zu = with_dep(z_s[sl, g * CG:(g + 1) * CG], dep)
                ug = _gelu(zu) * _silu(z_s[sl, 2 * D_C + g * CG:2 * D_C + (g + 1) * CG])
                ug_s[sl, cs] = ug
                dep = zero_of(ug[0:SUBLANES, 0:LANES])
            else:
                gv = _gelu(with_dep(z_s[sl, D_C + g * CG:D_C + (g + 1) * CG], dep))
                z_s[sl, D_C + g * CG:D_C + (g + 1) * CG] = gv
                s1 = s1 + jnp.sum(gv, axis=-1, keepdims=True)
                dep = zero_of(gv[0:SUBLANES, 0:LANES])
            if nxt is not None:
                while emitted < ((pi + 1) * n_dot) // n_piece:
                    d = proj_group(nxt, emitted)
                    dot_deps[pi] = d if pi not in dot_deps else dot_deps[pi] + d
                    emitted += 1
        for d in dot_deps.values():
            dep = dep + d
        c = z_s[sl, D_C:2 * D_C] - s1 * (1.0 / D_C)
        v = c * lax.rsqrt(jnp.mean(c * c, axis=-1, keepdims=True) + EPS) * lng_ref[...] + lnb_ref[...]
        vb_s[sl] = v.astype(_BF16)
        if tiles_per_seq is None:
            v_ref[rows] = v
        elif r == nblk - 1:
            rest[0][...] = v[RBLK - CHUNK_C:RBLK]
        for h in range(H_C):
            hs = slice(h * DG_C, (h + 1) * DG_C)
            for gr in range(RBLK // CHUNK_C):
                rs = slice(sl.start + gr * CHUNK_C, sl.start + (gr + 1) * CHUNK_C)
                mix = _dot(wms[h], vb_s[rs, hs]) + bmix_ref[h]
                p_s[rs, hs] = (ug_s[rs, hs] * mix).astype(_BF16)
        y = x_ref[rows] + _dot(p_s[sl], wout_ref[...])
        y_ref[rows] = _rms(y, fn_ref[...])
        return dep + zero_of(y[0:SUBLANES, 0:LANES])

    norm_rows(0)
    for k in range(n_dot):
        proj_group(0, k)
    dep = jnp.zeros((SUBLANES, LANES), _F32)
    for r in range(nblk):
        dep = back(r, dep, r + 1 if r + 1 < nblk else None)

    if tiles_per_seq is not None:
        @pl.when(i % tiles_per_seq == tiles_per_seq - 1)
        def _():
            v_ref[0, 0] = rest[0][...]


def _layer1(xr, gn, win, lng, lnb, wmix, bmix, wout, fn, *, R, seq_len):
    N = xr.shape[0]
    T = L1_SAMPLE_ROW_TILE if seq_len is None else L1_PROMPT_ROW_TILE
    srows = min(T, 2 * 256)
    if seq_len is None:
        tiles_per_seq = None
        v_shape = (N, D_C)
        v_spec = pl.BlockSpec((T, D_C), lambda i: (i, 0))
    else:
        tiles_per_seq = seq_len // T
        v_shape = (1, N // seq_len, CHUNK_C, D_C)
        v_spec = pl.BlockSpec((1, 1, CHUNK_C, D_C), lambda i: (0, i // tiles_per_seq, 0, 0))
    kern = functools.partial(_l1_kernel, T=T, R=R, tiles_per_seq=tiles_per_seq)
    return pl.pallas_call(
        kern,
        grid=(N // T,),
        in_specs=[
            pl.BlockSpec((T, D_MODEL), lambda i: (i, 0)),
            _const_spec((1, D_MODEL)),
            _const_spec((D_MODEL, 3 * D_C)),
            _const_spec((1, D_C)),
            _const_spec((1, D_C)),
            _const_spec(wmix.shape),
            _const_spec((H_C, CHUNK_C, 1)),
            _const_spec((D_C, D_MODEL)),
            _const_spec((1, D_MODEL)),
        ],
        out_specs=[pl.BlockSpec((T, D_MODEL), lambda i: (i, 0)), v_spec],
        out_shape=[jax.ShapeDtypeStruct((N, D_MODEL), _F32), jax.ShapeDtypeStruct(v_shape, _F32)],
        scratch_shapes=[
            pltpu.VMEM((srows, D_MODEL), _BF16),
            pltpu.VMEM((srows, 3 * D_C), _F32),
            pltpu.VMEM((srows, D_C), _BF16),
            pltpu.VMEM((srows, D_C), _F32),
            pltpu.VMEM((srows, D_C), _BF16),
        ] + ([] if seq_len is None else [pltpu.VMEM((CHUNK_C, D_C), _F32)]),
        compiler_params=pltpu.CompilerParams(
            dimension_semantics=("arbitrary",),
            vmem_limit_bytes=VMEM_LIMIT_BYTES),
        name="layer1_prompt" if seq_len is not None else "layer1_sample",
    )(xr, gn, win, lng, lnb, wmix, bmix, wout, fn)


def kernel(x_prompt, x_sample, state_conv, state_hgrn, norm_ab, w_in_ab, conv_w, conv_b, ln_a_g, ln_a_b, lb_logits, onorm_b, w_out_ab, norm_c, w_in_c, ln_c_g, ln_c_b, w_s, b_s, w_out_c, final_norm):
    B, L, _ = x_prompt.shape
    NB, LS, _ = x_sample.shape
    row = lambda p: p.reshape(1, -1)

    gn0 = row(norm_ab[0])
    win0 = w_in_ab[0].astype(_BF16)
    wout0 = w_out_ab[0].astype(_BF16)
    cw = jnp.pad(conv_w[0], ((0, HIST_ROWS - CONV_W), (0, 0)))
    cb, lng, lnb, og = row(conv_b[0]), row(ln_a_g[0]), row(ln_a_b[0]), row(onorm_b[0])
    l0_params = (gn0, win0, cw, cb, lng, lnb, lb_logits, og, wout0)

    xp1, conv_prompt, hgrn_prompt = _layer0_prompt(x_prompt, *l0_params)
    xs1, conv_sample, hgrn_sample = _layer0_sample(x_sample, state_conv, state_hgrn, *l0_params)

    gn1 = row(norm_c[0])
    win1 = w_in_c[0].astype(_BF16)
    wout1 = w_out_c[0].astype(_BF16)
    lcg, lcb, fn = row(ln_c_g[0]), row(ln_c_b[0]), row(final_norm)
    reps = CHUNK_C // LS
    wmix_p, bmix_p = w_s[0], b_s[0][:, :, None]
    wmix_s = w_s[0][:, :LS, :]
    bmix_s = jnp.tile(b_s[0][:, :LS], (1, reps))[:, :, None]

    y_prompt, v_prompt = _layer1(xp1.reshape(B * L, D_MODEL), gn1, win1, lcg, lcb, wmix_p, bmix_p, wout1, fn,
                                 R=CHUNK_C, seq_len=L)
    y_sample, v_sample = _layer1(xs1.reshape(NB * LS, D_MODEL), gn1, win1, lcg, lcb, wmix_s, bmix_s, wout1, fn,
                                 R=LS, seq_len=None)

    return (y_prompt.reshape(B, L, D_MODEL), y_sample.reshape(NB, LS, D_MODEL),
            conv_prompt, hgrn_prompt, v_prompt,
            conv_sample, hgrn_sample, v_sample.reshape(1, NB, LS, D_C))
```
